```python
import math
import jax, jax.numpy as jnp
from jax import lax
import numpy as np

D_MODEL = 1024
BATCH = 32
SEQ = 2048
DEPTH = 1
DEC_BATCH = 32
DEC_SEQ = 32
PAST_LEN = 2048

CHUNK = 64
Q_BLOCK = 128
H_A = 8
D_NOPE = 64
D_ROPE = 32
D_VA = 64
D_CQ = 256
D_CKV = 256
ROPE_BASE = 10000.0
H_B = 4
D_HB = 64
W_B = H_B * 2 * D_HB
D_MIX = H_A * D_VA + W_B
D_IN = D_CQ + D_CKV + D_ROPE + 3 * W_B
D_FF = 2816
CONV_W = 3
D_PLE = 256
ALPHA = (2 * DEPTH) ** 0.25
BETA = (8 * DEPTH) ** -0.25
EPS = 1e-5

kernel_name = "hymba_mla_diffattn_convglu_stream_step"


def _rms(x, g):
    xf = x.astype(jnp.float32)
    y = xf * lax.rsqrt(jnp.mean(xf * xf, -1, keepdims=True) + EPS)
    return (y * g.astype(jnp.float32)).astype(x.dtype)


def _ln(x, g, b):
    xf = x.astype(jnp.float32)
    mu = jnp.mean(xf, -1, keepdims=True)
    var = jnp.mean(jnp.square(xf - mu), -1, keepdims=True)
    y = (xf - mu) * lax.rsqrt(var + EPS)
    return (y * g.astype(jnp.float32) + b.astype(jnp.float32)).astype(x.dtype)


def _rope(x, pos):
    half = D_ROPE // 2
    inv = 1.0 / (ROPE_BASE ** (jnp.arange(half, dtype=jnp.float32) / half))
    ang = pos.astype(jnp.float32)[:, None] * inv[None, :]
    shape = (1, pos.shape[0]) + (1,) * (x.ndim - 3) + (half,)
    cos = jnp.cos(ang).reshape(shape)
    sin = jnp.sin(ang).reshape(shape)
    xf = x.astype(jnp.float32)
    x1, x2 = xf[..., :half], xf[..., half:]
    return jnp.concatenate([x1 * cos - x2 * sin, x1 * sin + x2 * cos], -1).astype(x.dtype)


def _alibi_slopes(n):
    return np.array([2.0 ** (-8.0 * (h + 1) / n) for h in range(n)], dtype=np.float32)


def _attend(q, k, v, q_pos, k_pos, slopes, coef, scale):
    b, sq, h, m, dk = q.shape
    bs = Q_BLOCK if sq % Q_BLOCK == 0 else sq
    nb = sq // bs
    k_chunk = k_pos // CHUNK
    coef32 = coef.astype(jnp.float32)

    def block(args):
        qb, pb = args
        s = jnp.einsum('bqhmd,bkhmd->bhmqk', qb, k).astype(jnp.float32) * scale
        if slopes is not None:
            dist = jnp.abs(pb[:, None] - k_pos[None, :]).astype(jnp.float32)
            s = s - slopes[None, :, None, None, None] * dist
        allowed = k_chunk[None, :] <= (pb // CHUNK)[:, None]
        s = jnp.where(allowed, s, -jnp.inf)
        p = jax.nn.softmax(s, axis=-1)
        w = jnp.einsum('bhmqk,hm->bhqk', p, coef32).astype(v.dtype)
        return jnp.einsum('bhqk,bkhd->bqhd', w, v)

    qs = jnp.moveaxis(q.reshape(b, nb, bs, h, m, dk), 1, 0)
    ps = q_pos.reshape(nb, bs)
    o = lax.map(block, (qs, ps))
    return jnp.moveaxis(o, 0, 1).reshape(b, sq, h, v.shape[-1])


def _token_mixers(x, pos, past, lp, layer_idx):
    b, s, _ = x.shape
    proj = x @ lp['w_in']
    o1 = D_CQ
    o2 = o1 + D_CKV
    o3 = o2 + D_ROPE
    o4 = o3 + W_B
    o5 = o4 + W_B
    c_q = _rms(proj[..., :o1], lp['g_cq'])
    c_kv = _rms(proj[..., o1:o2], lp['g_ckv'])
    k_r = _rope(proj[..., o2:o3], pos)
    dq = proj[..., o3:o4].reshape(b, s, H_B, 2, D_HB)
    dk = proj[..., o4:o5].reshape(b, s, H_B, 2, D_HB)
    dv = proj[..., o5:].reshape(b, s, H_B, 2 * D_HB)
    q = (c_q @ lp['w_uq']).reshape(b, s, H_A, D_NOPE + D_ROPE)
    q = jnp.concatenate([q[..., :D_NOPE], _rope(q[..., D_NOPE:], pos)], -1)
    if past is None:
        k_pos = pos
        c_all, kr_all, dk_all, dv_all = c_kv, k_r, dk, dv
    else:
        p_ckv, p_kr, p_dk, p_dv = past
        k_pos = jnp.arange(p_ckv.shape[1] + s)
        c_all = jnp.concatenate([p_ckv, c_kv], 1)
        kr_all = jnp.concatenate([p_kr, k_r], 1)
        dk_all = jnp.concatenate([p_dk, dk], 1)
        dv_all = jnp.concatenate([p_dv, dv], 1)
    sk = c_all.shape[1]
    k_nope = (c_all @ lp['w_uk']).reshape(b, sk, H_A, D_NOPE)
    v_a = (c_all @ lp['w_uv']).reshape(b, sk, H_A, D_VA)
    k_a = jnp.concatenate([k_nope, jnp.broadcast_to(kr_all[:, :, None, :], (b, sk, H_A, D_ROPE))], -1)
    o_a = _attend(q[:, :, :, None, :], k_a[:, :, :, None, :], v_a, pos, k_pos, None,
                  jnp.ones((H_A, 1), jnp.float32), (D_NOPE + D_ROPE) ** -0.5)
    lam_init = 0.8 - 0.6 * math.exp(-0.3 * layer_idx)
    f32 = jnp.float32
    lam = (jnp.exp(jnp.sum(lp['lambda_q1'].astype(f32) * lp['lambda_k1'].astype(f32)))
           - jnp.exp(jnp.sum(lp['lambda_q2'].astype(f32) * lp['lambda_k2'].astype(f32))) + lam_init)
    coef = jnp.broadcast_to(jnp.stack([jnp.ones_like(lam), -lam]), (H_B, 2))
    slopes = jnp.asarray(_alibi_slopes(H_B))
    o_b = _attend(dq, dk_all, dv_all, pos, k_pos, slopes, coef, D_HB ** -0.5)
    o_b = _rms(o_b, lp['g_subln']) * (1.0 - lam_init)
    mix = jnp.concatenate([o_a.reshape(b, s, H_A * D_VA), o_b.reshape(b, s, W_B)], -1)
    return mix @ lp['w_o'], (c_kv, k_r, dk, dv)


def _conv_ffn(h, conv_state, lp):
    b, s, _ = h.shape
    u = h @ lp['w_up']
    if conv_state is None:
        conv_state = jnp.zeros((b, CONV_W - 1, 2 * D_FF), u.dtype)
    up = jnp.concatenate([conv_state, u], 1)
    z = lp['conv_b'] + sum(up[:, j:j + s] * lp['conv_w'][j] for j in range(CONV_W))
    g, v = z[..., :D_FF], z[..., D_FF:]
    out = (jax.nn.gelu(g, approximate=False) * v) @ lp['w_down']
    return out, up[:, -(CONV_W - 1):]


def _layer(x, p_emb, pos, past, conv_state, lp, layer_idx):
    a, rows = _token_mixers(x, pos, past, lp, layer_idx)
    h = _ln(ALPHA * x + a, lp['ln1_g'], lp['ln1_b'])
    f, new_conv = _conv_ffn(h, conv_state, lp)
    h = _ln(ALPHA * h + f, lp['ln2_g'], lp['ln2_b'])
    gate = jax.nn.sigmoid(h @ lp['w_ple_gate'] + lp['b_ple_gate'])
    y = h + gate * (p_emb @ lp['w_ple_proj'])
    return y, rows + (new_conv,)


def setup_inputs(seed: int = 0) -> dict:
    key = jax.random.key(seed)
    ks = iter(jax.random.split(key, 48))

    def nrm(shape, scale=1.0):
        return jax.random.normal(next(ks), shape, jnp.float32) * scale

    L = DEPTH
    w_in = jnp.concatenate([nrm((L, D_MODEL, D_CQ + D_CKV + D_ROPE + 2 * W_B), D_MODEL ** -0.5),
                            nrm((L, D_MODEL, W_B), BETA * D_MODEL ** -0.5)], -1)
    return {
        'x_prompt': nrm((BATCH, SEQ, D_MODEL)),
        'x_sample': nrm((DEC_BATCH, DEC_SEQ, D_MODEL)),
        'cache_ckv': nrm((L, DEC_BATCH, PAST_LEN, D_CKV)),
        'cache_krope': nrm((L, DEC_BATCH, PAST_LEN, D_ROPE)),
        'cache_diff_k': nrm((L, DEC_BATCH, PAST_LEN, H_B, 2, D_HB)),
        'cache_diff_v': nrm((L, DEC_BATCH, PAST_LEN, H_B, 2 * D_HB), BETA),
        'state_ffn_conv': nrm((L, DEC_BATCH, CONV_W - 1, 2 * D_FF)),
        'p_prompt': nrm((L, BATCH, SEQ, D_PLE)),
        'p_sample': nrm((L, DEC_BATCH, DEC_SEQ, D_PLE)),
        'w_in': w_in,
        'g_cq': 1.0 + nrm((L, D_CQ), 0.01),
        'w_uq': nrm((L, D_CQ, H_A * (D_NOPE + D_ROPE)), D_CQ ** -0.5),
        'g_ckv': 1.0 + nrm((L, D_CKV), 0.01),
        'w_uk': nrm((L, D_CKV, H_A * D_NOPE), D_CKV ** -0.5),
        'w_uv': nrm((L, D_CKV, H_A * D_VA), BETA * D_CKV ** -0.5),
        'lambda_q1': nrm((L, D_HB), 0.1),
        'lambda_k1': nrm((L, D_HB), 0.1),
        'lambda_q2': nrm((L, D_HB), 0.1),
        'lambda_k2': nrm((L, D_HB), 0.1),
        'g_subln': 1.0 + nrm((L, 2 * D_HB), 0.01),
        'w_o': nrm((L, D_MIX, D_MODEL), BETA * D_MIX ** -0.5),
        'ln1_g': 1.0 + nrm((L, D_MODEL), 0.01),
        'ln1_b': nrm((L, D_MODEL), 0.01),
        'w_up': nrm((L, D_MODEL, 2 * D_FF), D_MODEL ** -0.5),
        'conv_w': nrm((L, CONV_W, 2 * D_FF), CONV_W ** -0.5),
        'conv_b': nrm((L, 2 * D_FF), 0.01),
        'w_down': nrm((L, D_FF, D_MODEL), BETA * D_FF ** -0.5),
        'ln2_g': 1.0 + nrm((L, D_MODEL), 0.01),
        'ln2_b': nrm((L, D_MODEL), 0.01),
        'w_ple_gate': nrm((L, D_MODEL, D_MODEL), D_MODEL ** -0.5),
        'b_ple_gate': nrm((L, D_MODEL), 0.01),
        'w_ple_proj': nrm((L, D_PLE, D_MODEL), D_PLE ** -0.5),
    }


def reference(x_prompt, x_sample, cache_ckv, cache_krope, cache_diff_k, cache_diff_v, state_ffn_conv,
              p_prompt, p_sample, w_in, g_cq, w_uq, g_ckv, w_uk, w_uv, lambda_q1, lambda_k1,
              lambda_q2, lambda_k2, g_subln, w_o, ln1_g, ln1_b, w_up, conv_w, conv_b, w_down,
              ln2_g, ln2_b, w_ple_gate, b_ple_gate, w_ple_proj):
    past_len = cache_ckv.shape[2]
    pos_p = jnp.arange(x_prompt.shape[1])
    pos_s = past_len + jnp.arange(x_sample.shape[1])
    y_p, y_s = x_prompt, x_sample
    new_p, new_s = [], []
    for i in range(DEPTH):
        lp = {'w_in': w_in[i], 'g_cq': g_cq[i], 'w_uq': w_uq[i], 'g_ckv': g_ckv[i], 'w_uk': w_uk[i],
              'w_uv': w_uv[i], 'lambda_q1': lambda_q1[i], 'lambda_k1': lambda_k1[i],
              'lambda_q2': lambda_q2[i], 'lambda_k2': lambda_k2[i], 'g_subln': g_subln[i],
              'w_o': w_o[i], 'ln1_g': ln1_g[i], 'ln1_b': ln1_b[i], 'w_up': w_up[i],
              'conv_w': conv_w[i], 'conv_b': conv_b[i], 'w_down': w_down[i], 'ln2_g': ln2_g[i],
              'ln2_b': ln2_b[i], 'w_ple_gate': w_ple_gate[i], 'b_ple_gate': b_ple_gate[i],
              'w_ple_proj': w_ple_proj[i]}
        y_p, st_p = _layer(y_p, p_prompt[i], pos_p, None, None, lp, i)
        past = (cache_ckv[i], cache_krope[i], cache_diff_k[i], cache_diff_v[i])
        y_s, st_s = _layer(y_s, p_sample[i], pos_s, past, state_ffn_conv[i], lp, i)
        new_p.append(st_p)
        new_s.append(st_s)
    sp = [jnp.stack([st[j] for st in new_p]) for j in range(5)]
    ss = [jnp.stack([st[j] for st in new_s]) for j in range(5)]
    return (y_p, y_s, sp[0], sp[1], sp[2], sp[3], sp[4], ss[0], ss[1], ss[2], ss[3], ss[4])
```

```python
import functools
import math

import numpy as np
import jax
import jax.numpy as jnp
from jax import lax
from jax.experimental import pallas as pl
from jax.experimental.pallas import tpu as pltpu

F32 = jnp.float32
BF16 = jnp.bfloat16

D_MODEL = 1024
DEPTH = 1
CHUNK = 64
H_A = 8
D_NOPE = 64
D_ROPE = 32
D_VA = 64
D_CQ = 256
D_CKV = 256
ROPE_BASE = 10000.0
H_B = 4
D_HB = 64
W_B = H_B * 2 * D_HB
D_FF = 2816
CONV_W = 3
D_PLE = 256
ALPHA = (2 * DEPTH) ** 0.25
EPS = 1e-5
LAM_INIT = 0.8 - 0.6 * math.exp(-0.3 * 0)
SCALE_A = (D_NOPE + D_ROPE) ** -0.5
SCALE_B = D_HB ** -0.5

LANES = 128
VMEM_LIMIT = 56 * 1024 * 1024
W1_COLS = 2048 + 2 * LANES
TQ = 256
FC = 256
N_FC = D_FF // FC


def _dot(a, b):
    return jnp.dot(a, b, preferred_element_type=F32)


def _dot_nt(a, b):
    return lax.dot_general(a, b, (((1,), (1,)), ((), ())), preferred_element_type=F32)


def _const_spec(shape):
    nd = len(shape)
    return pl.BlockSpec(shape, lambda *_: (0,) * nd, pipeline_mode=pl.Buffered(1))


def _rms(v, g):
    return v * lax.rsqrt(jnp.mean(v * v, -1, keepdims=True) + EPS) * g


def _gelu(v):
    return 0.5 * v * (1.0 + lax.erf(v * (2.0 ** -0.5)))


def _ln(v, g, b):
    mu = jnp.mean(v, -1, keepdims=True)
    d = v - mu
    var = jnp.mean(d * d, -1, keepdims=True)
    return d * lax.rsqrt(var + EPS) * g + b


def _pre_kernel(x_ref, ct_ref, st_ref, w1_ref, wq_ref, wk_ref, wv_ref, gcq_ref, gckv_ref,
                ckv_ref, krope_ref, ndk_ref, ndv_ref,
                qa_ref, ka_ref, va_ref, dq_ref, dk_ref, dv_ref):
    proj = _dot(x_ref[...].astype(BF16), w1_ref[...])
    cqn = _rms(proj[:, 0:D_CQ], gcq_ref[...])
    ckvn = _rms(proj[:, D_CQ:D_CQ + D_CKV], gckv_ref[...])
    ckv_ref[...] = ckvn
    dq = proj[:, 512:1024]
    dk = proj[:, 1024:1536]
    dv = proj[:, 1536:2048]
    ndk_ref[...] = dk
    ndv_ref[...] = dv
    dq_ref[...] = (dq * SCALE_B).astype(BF16)
    dk_ref[...] = dk.astype(BF16)
    dv_ref[...] = dv.astype(BF16)
    ct = ct_ref[...]
    st = st_ref[...]
    krp = proj[:, 2048:2048 + LANES] * ct + proj[:, 2048 + LANES:2048 + 2 * LANES] * st
    krope_ref[...] = pltpu.roll(krp, 64, 1)[:, 0:D_ROPE]
    ckvb = ckvn.astype(BF16)
    kn = _dot(ckvb, wk_ref[...])
    va_ref[...] = _dot(ckvb, wv_ref[...]).astype(BF16)
    qq = _dot(cqn.astype(BF16), wq_ref[...])
    for h in range(H_A):
        sl = slice(h * LANES, (h + 1) * LANES)
        ka_ref[:, sl] = (kn[:, sl] + krp).astype(BF16)
        qh = qq[:, sl] * ct + qq[:, H_A * LANES + h * LANES:H_A * LANES + (h + 1) * LANES] * st
        qa_ref[:, sl] = (qh * SCALE_A).astype(BF16)


def _pre_call(x2d, ct, st, w1, wq, wk, wv, gcq, gckv, tm, n_tab):
    t = x2d.shape[0]
    grid = (t // tm,)
    row = lambda c: pl.BlockSpec((tm, c), lambda i: (i, 0))
    tab = pl.BlockSpec((tm, LANES), lambda i: (i % n_tab, 0))
    out_shapes = (
        jax.ShapeDtypeStruct((t, D_CKV), F32), jax.ShapeDtypeStruct((t, D_ROPE), F32),
        jax.ShapeDtypeStruct((t, W_B), F32), jax.ShapeDtypeStruct((t, W_B), F32),
        jax.ShapeDtypeStruct((t, H_A * LANES), BF16), jax.ShapeDtypeStruct((t, H_A * LANES), BF16),
        jax.ShapeDtypeStruct((t, H_A * D_VA), BF16), jax.ShapeDtypeStruct((t, W_B), BF16),
        jax.ShapeDtypeStruct((t, W_B), BF16), jax.ShapeDtypeStruct((t, W_B), BF16))
    return pl.pallas_call(
        _pre_kernel,
        grid=grid,
        in_specs=[row(D_MODEL), tab, tab, _const_spec(w1.shape), _const_spec(wq.shape),
                  _const_spec(wk.shape), _const_spec(wv.shape), _const_spec(gcq.shape),
                  _const_spec(gckv.shape)],
        out_specs=(row(D_CKV), row(D_ROPE), row(W_B), row(W_B), row(H_A * LANES), row(H_A * LANES),
                   row(H_A * D_VA), row(W_B), row(W_B), row(W_B)),
        out_shape=out_shapes,
        compiler_params=pltpu.CompilerParams(dimension_semantics=("arbitrary",),
                                             vmem_limit_bytes=VMEM_LIMIT),
        name="pre_proj",
    )(x2d, ct, st, w1, wq, wk, wv, gcq, gckv)


def _diag_mask():
    r = lax.broadcasted_iota(jnp.int32, (TQ, TQ), 0) // CHUNK
    c = lax.broadcasted_iota(jnp.int32, (TQ, TQ), 1) // CHUNK
    return c <= r


def _tile_attend(q, k_ref, v_ref, r0, mask, bias_d, bias_off_fn):
    sd = _dot_nt(q, k_ref[0, r0:r0 + TQ, :])
    if bias_d is not None:
        sd = sd + bias_d
    sd = jnp.where(mask, sd, -jnp.inf)
    m = jnp.max(sd, -1, keepdims=True)
    if r0 > 0:
        so = _dot_nt(q, k_ref[0, 0:r0, :])
        if bias_off_fn is not None:
            so = so + bias_off_fn(r0)
        m = jnp.maximum(m, jnp.max(so, -1, keepdims=True))
        po = jnp.exp(so - m)
        l = jnp.sum(po, -1, keepdims=True)
        o = _dot(po.astype(BF16), v_ref[0, 0:r0, :])
    pd = jnp.exp(sd - m)
    if r0 > 0:
        l = l + jnp.sum(pd, -1, keepdims=True)
        o = o + _dot(pd.astype(BF16), v_ref[0, r0:r0 + TQ, :])
    else:
        l = jnp.sum(pd, -1, keepdims=True)
        o = _dot(pd.astype(BF16), v_ref[0, r0:r0 + TQ, :])
    return o, l


def _attn_a_kernel(q1_ref, q2_ref, k1_ref, k2_ref, v_ref, o_ref):
    s = q1_ref.shape[1]
    mask = _diag_mask()
    lane = lax.broadcasted_iota(jnp.int32, (TQ, LANES), 1)
    for qi in range(s // TQ):
        r0 = qi * TQ
        o1, l1 = _tile_attend(q1_ref[0, r0:r0 + TQ, :], k1_ref, v_ref, r0, mask, None, None)
        o2, l2 = _tile_attend(q2_ref[0, r0:r0 + TQ, :], k2_ref, v_ref, r0, mask, None, None)
        o_ref[0, r0:r0 + TQ, :] = jnp.where(lane < D_VA, o1 / l1, o2 / l2).astype(BF16)


def _lambda_value(lq1_ref, lk1_ref, lq2_ref, lk2_ref):
    a = jnp.sum(lq1_ref[...] * lk1_ref[...], -1, keepdims=True)
    b = jnp.sum(lq2_ref[...] * lk2_ref[...], -1, keepdims=True)
    return jnp.exp(a) - jnp.exp(b) + LAM_INIT


def _attn_b_kernel(slope_ref, lq1_ref, lk1_ref, lq2_ref, lk2_ref, g_ref, q_ref, k_ref, v_ref, o_ref):
    s = q_ref.shape[1]
    mask = _diag_mask()
    lane = lax.broadcasted_iota(jnp.int32, (TQ, LANES), 1)
    slope = slope_ref[0, 0:1, 0:1]
    lam = _lambda_value(lq1_ref, lk1_ref, lq2_ref, lk2_ref)
    ri = lax.broadcasted_iota(jnp.int32, (TQ, TQ), 0)
    ci = lax.broadcasted_iota(jnp.int32, (TQ, TQ), 1)
    bias_d = -slope * jnp.abs(ri - ci).astype(F32)
    rowf = lax.broadcasted_iota(jnp.int32, (TQ, 1), 0).astype(F32)

    def bias_off(r0):
        colf = lax.broadcasted_iota(jnp.int32, (1, r0), 1).astype(F32)
        return slope * colf - slope * (rowf + float(r0))

    for qi in range(s // TQ):
        r0 = qi * TQ
        q = q_ref[0, r0:r0 + TQ, :]
        q1 = jnp.where(lane < D_HB, q, jnp.zeros_like(q))
        q2 = jnp.where(lane >= D_HB, q, jnp.zeros_like(q))
        o1, l1 = _tile_attend(q1, k_ref, v_ref, r0, mask, bias_d, bias_off)
        o2, l2 = _tile_attend(q2, k_ref, v_ref, r0, mask, bias_d, bias_off)
        o = o1 / l1 - lam * (o2 / l2)
        o_ref[0, r0:r0 + TQ, :] = (_rms(o, g_ref[...]) * (1.0 - LAM_INIT)).astype(BF16)


def _attn_a_call(qa, ka, va):
    b, s, _ = qa.shape
    blk = lambda f: pl.BlockSpec((1, s, LANES), f)
    return pl.pallas_call(
        _attn_a_kernel,
        grid=(b, H_A // 2),
        in_specs=[blk(lambda i, j: (i, 0, 2 * j)), blk(lambda i, j: (i, 0, 2 * j + 1)),
                  blk(lambda i, j: (i, 0, 2 * j)), blk(lambda i, j: (i, 0, 2 * j + 1)),
                  blk(lambda i, j: (i, 0, j))],
        out_specs=blk(lambda i, j: (i, 0, j)),
        out_shape=jax.ShapeDtypeStruct((b, s, H_A * D_VA), BF16),
        compiler_params=pltpu.CompilerParams(dimension_semantics=("arbitrary", "arbitrary"),
                                             vmem_limit_bytes=VMEM_LIMIT),
        name="attn_mla",
    )(qa, qa, ka, ka, va)


def _attn_b_call(slopes, lq1, lk1, lq2, lk2, g, dq, dk, dv):
    b, s, _ = dq.shape
    blk = pl.BlockSpec((1, s, LANES), lambda i, j: (i, 0, j))
    vec = lambda a: pl.BlockSpec(a.shape, lambda i, j: (0,) * a.ndim)
    return pl.pallas_call(
        _attn_b_kernel,
        grid=(b, H_B),
        in_specs=[pl.BlockSpec((1, 8, LANES), lambda i, j: (j, 0, 0)),
                  vec(lq1), vec(lk1), vec(lq2), vec(lk2), vec(g), blk, blk, blk],
        out_specs=blk,
        out_shape=jax.ShapeDtypeStruct((b, s, W_B), BF16),
        compiler_params=pltpu.CompilerParams(dimension_semantics=("arbitrary", "arbitrary"),
                                             vmem_limit_bytes=VMEM_LIMIT),
        name="attn_diff",
    )(slopes, lq1, lk1, lq2, lk2, g, dq, dk, dv)


def _stack_rows(x, reps, width):
    n, c = x.shape
    t = jnp.concatenate([x] * reps, axis=0)
    rb = lax.broadcasted_iota(jnp.int32, (reps * n, c), 0) // n
    cb = lax.broadcasted_iota(jnp.int32, (reps * n, c), 1) // width
    return jnp.where(rb == cb, t, jnp.zeros_like(t))


def _softmax2(sc, sn):
    m = jnp.maximum(jnp.max(sc, -1, keepdims=True), jnp.max(sn, -1, keepdims=True))
    pc = jnp.exp(sc - m)
    pn = jnp.exp(sn - m)
    l = jnp.sum(pc, -1, keepdims=True) + jnp.sum(pn, -1, keepdims=True)
    return pc.astype(BF16), pn.astype(BF16), l


def _attn_s_kernel(past_len, qa_ref, dq_ref, cckv_ref, ckr_ref, cdk_ref, cdv_ref,
                   nckv_ref, nkr_ref, ndk_ref, ndv_ref, wabs_ref, wuvb_ref,
                   lq1_ref, lk1_ref, lq2_ref, lk2_ref, g_ref, oa_ref, ob_ref):
    n = qa_ref.shape[1]
    qblk = _stack_rows(qa_ref[0], H_A, LANES)
    g = _dot(qblk, wabs_ref[...])
    qabs = g[:, 0:D_CKV].astype(BF16)
    qrp = g[:, D_CKV:D_CKV + D_ROPE].astype(BF16)
    cc = cckv_ref[0].astype(BF16)
    cn = nckv_ref[0].astype(BF16)
    sc = _dot_nt(qabs, cc) + _dot_nt(qrp, ckr_ref[0].astype(BF16))
    sn = _dot_nt(qabs, cn) + _dot_nt(qrp, nkr_ref[0].astype(BF16))
    pc, pn, l = _softmax2(sc, sn)
    ctx = ((_dot(pc, cc) + _dot(pn, cn)) / l).astype(BF16)
    oa = _dot(ctx[0:n], wuvb_ref[0:D_CKV, :])
    for h in range(1, H_A):
        oa = oa + _dot(ctx[h * n:(h + 1) * n], wuvb_ref[h * D_CKV:(h + 1) * D_CKV, :])
    oa_ref[0] = oa.astype(BF16)
    rows = 2 * H_B * n
    qb = _stack_rows(dq_ref[0], 2 * H_B, D_HB)
    sc = _dot_nt(qb, cdk_ref[0].astype(BF16))
    sn = _dot_nt(qb, ndk_ref[0])
    r = lax.broadcasted_iota(jnp.int32, (rows, 1), 0)
    hh = r // (2 * n)
    slope = jnp.where(hh == 0, 2.0 ** -2, jnp.where(hh == 1, 2.0 ** -4,
                      jnp.where(hh == 2, 2.0 ** -6, 2.0 ** -8))).astype(F32)
    qpos = (r % n).astype(F32)
    kc = lax.broadcasted_iota(jnp.int32, (1, past_len), 1).astype(F32)
    kn = lax.broadcasted_iota(jnp.int32, (1, n), 1).astype(F32)
    sc = sc - slope * jnp.abs((qpos + float(past_len)) - kc)
    sn = sn - slope * jnp.abs(qpos - kn)
    pc, pn, l = _softmax2(sc, sn)
    of = (_dot(pc, cdv_ref[0].astype(BF16)) + _dot(pn, ndv_ref[0])) / l
    lam = _lambda_value(lq1_ref, lk1_ref, lq2_ref, lk2_ref)
    for h in range(H_B):
        sl = slice(h * LANES, (h + 1) * LANES)
        o = of[2 * h * n:(2 * h + 1) * n, sl] - lam * of[(2 * h + 1) * n:(2 * h + 2) * n, sl]
        ob_ref[0, :, sl] = (_rms(o, g_ref[...]) * (1.0 - LAM_INIT)).astype(BF16)


def _attn_s_call(qa, dq, cckv, ckr, cdk, cdv, nckv, nkr, ndk, ndv, wabs, wuvb, lq1, lk1, lq2, lk2, g):
    b, n, _ = qa.shape
    past_len = cckv.shape[1]
    per_b = lambda a: pl.BlockSpec((1,) + a.shape[1:], lambda i: (i, 0, 0))
    vec = lambda a: pl.BlockSpec(a.shape, lambda i: (0,) * a.ndim)
    args = (qa, dq, cckv, ckr, cdk, cdv, nckv, nkr, ndk, ndv)
    consts = (wabs, wuvb)
    vecs = (lq1, lk1, lq2, lk2, g)
    return pl.pallas_call(
        functools.partial(_attn_s_kernel, past_len),
        grid=(b,),
        in_specs=[per_b(a) for a in args] + [_const_spec(a.shape) for a in consts] + [vec(a) for a in vecs],
        out_specs=(pl.BlockSpec((1, n, H_A * D_VA), lambda i: (i, 0, 0)),
                   pl.BlockSpec((1, n, W_B), lambda i: (i, 0, 0))),
        out_shape=(jax.ShapeDtypeStruct((b, n, H_A * D_VA), BF16), jax.ShapeDtypeStruct((b, n, W_B), BF16)),
        compiler_params=pltpu.CompilerParams(dimension_semantics=("arbitrary",),
                                             vmem_limit_bytes=VMEM_LIMIT),
        name="attn_sample",
    )(*args, *consts, *vecs)


def _post_kernel(nb, tiles_per_seq, x_ref, oa_ref, ob_ref, p_ref, state_ref,
                 woa_ref, wob_ref, ln1g_ref, ln1b_ref, wup_ref, cw_ref, cb_ref, wdn_ref,
                 ln2g_ref, ln2b_ref, wg_ref, bg_ref, wp_ref,
                 y_ref, nconv_ref, carry_ref, upad_ref, act_ref):
    tm = x_ref.shape[0]
    s = tm // nb
    i = pl.program_id(0)

    @pl.when(i % tiles_per_seq == 0)
    def _():
        carry_ref[...] = state_ref[...]

    a = _dot(oa_ref[...], woa_ref[...]) + _dot(ob_ref[...], wob_ref[...])
    h = _ln(ALPHA * x_ref[...] + a, ln1g_ref[...], ln1b_ref[...])
    hb = h.astype(BF16)

    def conv(c0):
        u = _dot(hb, wup_ref[:, c0:c0 + FC])
        u3 = u.reshape(nb, s, FC)
        upad_ref[:, 6:8, :] = carry_ref[:, :, c0:c0 + FC]
        upad_ref[:, 8:8 + s, :] = u3
        tail = u3[:, s - 2:s, :]
        carry_ref[:, :, c0:c0 + FC] = tail
        nconv_ref[:, :, c0:c0 + FC] = tail
        z = (cb_ref[:, c0:c0 + FC]
             + upad_ref[:, 6:6 + s, :] * cw_ref[0:1, c0:c0 + FC]
             + upad_ref[:, 7:7 + s, :] * cw_ref[1:2, c0:c0 + FC]
             + u3 * cw_ref[2:3, c0:c0 + FC])
        return z.reshape(tm, FC)

    for c in range(N_FC):
        zg = conv(c * FC)
        zv = conv(D_FF + c * FC)
        act_ref[:, c * FC:(c + 1) * FC] = (_gelu(zg) * zv).astype(BF16)

    f = _dot(act_ref[...], wdn_ref[...])
    h2 = _ln(ALPHA * h + f, ln2g_ref[...], ln2b_ref[...])
    gate = jax.nn.sigmoid(_dot(h2.astype(BF16), wg_ref[...]) + bg_ref[...])
    y_ref[...] = h2 + gate * _dot(p_ref[...].astype(BF16), wp_ref[...])


def _post_call(x2d, oa, ob, p2d, state, weights, tm, nb, tiles_per_seq):
    t = x2d.shape[0]
    n_seq = state.shape[0]
    s = tm // nb
    row = lambda c: pl.BlockSpec((tm, c), lambda i: (i, 0))
    st_spec = pl.BlockSpec((nb, CONV_W - 1, 2 * D_FF), lambda i: (i // tiles_per_seq, 0, 0))
    return pl.pallas_call(
        functools.partial(_post_kernel, nb, tiles_per_seq),
        grid=(t // tm,),
        in_specs=[row(D_MODEL), row(H_A * D_VA), row(W_B), row(D_PLE), st_spec]
                 + [_const_spec(w.shape) for w in weights],
        out_specs=(row(D_MODEL), st_spec),
        out_shape=(jax.ShapeDtypeStruct((t, D_MODEL), F32),
                   jax.ShapeDtypeStruct((n_seq, CONV_W - 1, 2 * D_FF), F32)),
        scratch_shapes=[pltpu.VMEM((nb, CONV_W - 1, 2 * D_FF), F32),
                        pltpu.VMEM((nb, s + 8, FC), F32),
                        pltpu.VMEM((tm, D_FF), BF16)],
        compiler_params=pltpu.CompilerParams(dimension_semantics=("arbitrary",),
                                             vmem_limit_bytes=VMEM_LIMIT),
        name="post_ffn",
    )(x2d, oa, ob, p2d, state, *weights)


def _rope_tables(pos):
    half = D_ROPE // 2
    inv = 1.0 / (ROPE_BASE ** (jnp.arange(half, dtype=F32) / half))
    ang = pos.astype(F32)[:, None] * inv[None, :]
    cos, sin = jnp.cos(ang), jnp.sin(ang)
    n = pos.shape[0]
    ct = jnp.concatenate([jnp.ones((n, D_NOPE), F32), cos, cos, jnp.zeros((n, LANES - D_NOPE - D_ROPE), F32)], 1)
    st = jnp.concatenate([jnp.zeros((n, D_NOPE), F32), sin, sin, jnp.zeros((n, LANES - D_NOPE - D_ROPE), F32)], 1)
    return ct, st


def _rot_half_cols(w):
    half = D_ROPE // 2
    return jnp.concatenate([-w[..., half:], w[..., :half]], -1)


def _prep_weights(w_in, w_uq, w_uk, w_uv):
    o1, o2, o3 = D_CQ, D_CQ + D_CKV, D_CQ + D_CKV + D_ROPE
    w_kr = w_in[:, o2:o3]
    z = lambda c: jnp.zeros((D_MODEL, c), F32)
    w1 = jnp.concatenate([w_in[:, :o2], w_in[:, o3:],
                          z(D_NOPE), w_kr, z(LANES - D_NOPE - D_ROPE),
                          z(D_NOPE), _rot_half_cols(w_kr), z(LANES - D_NOPE - D_ROPE)], 1).astype(BF16)
    wq3 = w_uq.reshape(D_CQ, H_A, D_NOPE + D_ROPE)
    zq = lambda c: jnp.zeros((D_CQ, H_A, c), F32)
    wq_a = jnp.concatenate([wq3, zq(LANES - D_NOPE - D_ROPE)], -1).reshape(D_CQ, H_A * LANES)
    wq_b = jnp.concatenate([zq(D_NOPE), _rot_half_cols(wq3[..., D_NOPE:]), zq(LANES - D_NOPE - D_ROPE)],
                           -1).reshape(D_CQ, H_A * LANES)
    wq = jnp.concatenate([wq_a, wq_b], 1).astype(BF16)
    wk3 = w_uk.reshape(D_CKV, H_A, D_NOPE)
    wk = jnp.concatenate([wk3, jnp.zeros((D_CKV, H_A, LANES - D_NOPE), F32)], -1).reshape(D_CKV, H_A * LANES)
    wabs_lat = jnp.concatenate([jnp.transpose(wk3, (1, 2, 0)),
                                jnp.zeros((H_A, LANES - D_NOPE, D_CKV), F32)], 1)
    pick = jnp.zeros((LANES, LANES), F32).at[D_NOPE + jnp.arange(D_ROPE), jnp.arange(D_ROPE)].set(1.0)
    wabs = jnp.concatenate([wabs_lat, jnp.broadcast_to(pick, (H_A, LANES, LANES))], -1)
    wabs = wabs.reshape(H_A * LANES, D_CKV + LANES).astype(BF16)
    wv3 = w_uv.reshape(D_CKV, H_A, D_VA)
    eye = jnp.eye(H_A, dtype=F32)
    wuvb = jnp.einsum('lhd,hg->hlgd', wv3, eye).reshape(H_A * D_CKV, H_A * D_VA).astype(BF16)
    return w1, wq, wk.astype(BF16), w_uv.astype(BF16), wabs, wuvb


def kernel(x_prompt, x_sample, cache_ckv, cache_krope, cache_diff_k, cache_diff_v, state_ffn_conv, p_prompt, p_sample, w_in, g_cq, w_uq, g_ckv, w_uk, w_uv, lambda_q1, lambda_k1, lambda_q2, lambda_k2, g_subln, w_o, ln1_g, ln1_b, w_up, conv_w, conv_b, w_down, ln2_g, ln2_b, w_ple_gate, b_ple_gate, w_ple_proj):
    b, s, _ = x_prompt.shape
    bs, n, _ = x_sample.shape
    past_len = cache_ckv.shape[2]
    w1, wq, wk, wv, wabs, wuvb = _prep_weights(w_in[0], w_uq[0], w_uk[0], w_uv[0])
    gcq, gckv = g_cq, g_ckv
    post_w = (w_o[0][:H_A * D_VA].astype(BF16), w_o[0][H_A * D_VA:].astype(BF16), ln1_g, ln1_b,
              w_up[0].astype(BF16), conv_w[0], conv_b, w_down[0].astype(BF16), ln2_g, ln2_b,
              w_ple_gate[0].astype(BF16), b_ple_gate, w_ple_proj[0].astype(BF16))
    slopes = jnp.broadcast_to(
        jnp.asarray([2.0 ** (-8.0 * (h + 1) / H_B) for h in range(H_B)], F32)[:, None, None], (H_B, 8, LANES))

    tm = 512
    ct, st = _rope_tables(jnp.arange(s))
    (ckv_p, kr_p, ndk_p, ndv_p, qa, ka, va, dq, dk, dv) = _pre_call(
        x_prompt.reshape(b * s, D_MODEL), ct, st, w1, wq, wk, wv, gcq, gckv, tm, s // tm)
    r3 = lambda a: a.reshape(b, s, a.shape[-1])
    oa = _attn_a_call(r3(qa), r3(ka), r3(va))
    ob = _attn_b_call(slopes, lambda_q1, lambda_k1, lambda_q2, lambda_k2, g_subln, r3(dq), r3(dk), r3(dv))
    zero_state = jnp.zeros((b, CONV_W - 1, 2 * D_FF), F32)
    y_p, conv_p = _post_call(x_prompt.reshape(b * s, D_MODEL), oa.reshape(b * s, -1), ob.reshape(b * s, -1),
                             p_prompt[0].reshape(b * s, D_PLE), zero_state, post_w, tm, 1, s // tm)

    nbs = 8
    tms = nbs * n
    cts, sts = _rope_tables(past_len + jnp.arange(n))
    cts, sts = jnp.tile(cts, (nbs, 1)), jnp.tile(sts, (nbs, 1))
    (ckv_s, kr_s, ndk_s, ndv_s, qa_s, _, _, dq_s, dk_s, dv_s) = _pre_call(
        x_sample.reshape(bs * n, D_MODEL), cts, sts, w1, wq, wk, wv, gcq, gckv, tms, 1)
    q3 = lambda a: a.reshape(bs, n, a.shape[-1])
    oa_s, ob_s = _attn_s_call(
        q3(qa_s), q3(dq_s), cache_ckv[0], cache_krope[0],
        cache_diff_k[0].reshape(bs, past_len, W_B), cache_diff_v[0].reshape(bs, past_len, W_B),
        q3(ckv_s), q3(kr_s), q3(dk_s), q3(dv_s), wabs, wuvb,
        lambda_q1, lambda_k1, lambda_q2, lambda_k2, g_subln)
    y_s, conv_s = _post_call(x_sample.reshape(bs * n, D_MODEL), oa_s.reshape(bs * n, -1),
                             ob_s.reshape(bs * n, -1), p_sample[0].reshape(bs * n, D_PLE),
                             state_ffn_conv[0], post_w, tms, nbs, 1)

    return (y_p.reshape(b, s, D_MODEL), y_s.reshape(bs, n, D_MODEL),
            ckv_p.reshape(1, b, s, D_CKV), kr_p.reshape(1, b, s, D_ROPE),
            ndk_p.reshape(1, b, s, H_B, 2, D_HB), ndv_p.reshape(1, b, s, H_B, 2 * D_HB),
            conv_p[None],
            ckv_s.reshape(1, bs, n, D_CKV), kr_s.reshape(1, bs, n, D_ROPE),
            ndk_s.reshape(1, bs, n, H_B, 2, D_HB), ndv_s.reshape(1, bs, n, H_B, 2 * D_HB),
            conv_s[None])
```

```python
import functools
import math

import numpy as np
import jax
import jax.numpy as jnp
from jax import lax
from jax.experimental import pallas as pl
from jax.experimental.pallas import tpu as pltpu

F32 = jnp.float32
BF16 = jnp.bfloat16

D_MODEL = 1024
DEPTH = 1
CHUNK = 64
H_A = 8
D_NOPE = 64
D_ROPE = 32
D_VA = 64
D_CQ = 256
D_CKV = 256
ROPE_BASE = 10000.0
H_B = 4
D_HB = 64
W_B = H_B * 2 * D_HB
D_FF = 2816
CONV_W = 3
D_PLE = 256
ALPHA = (2 * DEPTH) ** 0.25
EPS = 1e-5
LAM_INIT = 0.8 - 0.6 * math.exp(-0.3 * 0)
LOG2E = 1.4426950408889634
SCALE_A = (D_NOPE + D_ROPE) ** -0.5 * LOG2E
SCALE_B = D_HB ** -0.5 * LOG2E

LANES = 128
VMEM_LIMIT = 56 * 1024 * 1024
TQ = 512
HQ = TQ // 2
FC = 256
N_FC = D_FF // FC
N_AUG = 6


def _dot(a, b):
    return jnp.dot(a, b, preferred_element_type=F32)


def _dot_nt(a, b):
    return lax.dot_general(a, b, (((1,), (1,)), ((), ())), preferred_element_type=F32)


def _const_spec(shape):
    nd = len(shape)
    return pl.BlockSpec(shape, lambda *_: (0,) * nd, pipeline_mode=pl.Buffered(1))


def _rms(v, g):
    return v * lax.rsqrt(jnp.mean(v * v, -1, keepdims=True) + EPS) * g


def _gelu(v):
    return 0.5 * v * (1.0 + lax.erf(v * (2.0 ** -0.5)))


def _ln(v, g, b):
    mu = jnp.mean(v, -1, keepdims=True)
    d = v - mu
    var = jnp.mean(d * d, -1, keepdims=True)
    return d * lax.rsqrt(var + EPS) * g + b


def _rope_q(qq, ct, st, qa_ref):
    for h in range(H_A):
        sl = slice(h * LANES, (h + 1) * LANES)
        qh = qq[:, sl] * ct + qq[:, H_A * LANES + h * LANES:H_A * LANES + (h + 1) * LANES] * st
        qa_ref[:, sl] = (qh * SCALE_A).astype(BF16)


def _pre_t_kernel(x_ref, ct_ref, st_ref, ctt_ref, stt_ref, w1_ref, wt_ref, wq_ref, wukt_ref, wv_ref,
                  gcq_ref, gckv_ref,
                  ckv_ref, krt_ref, dkt_ref, dv4_ref,
                  qa_ref, knt_ref, krtb_ref, va_ref, dq_ref, dktb_ref, dvb_ref):
    tm = x_ref.shape[0]
    xb = x_ref[...].astype(BF16)
    proj = _dot(xb, w1_ref[...])
    projt = _dot_nt(wt_ref[...], xb)
    cqn = _rms(proj[:, 0:D_CQ], gcq_ref[...])
    ckvn = _rms(proj[:, D_CQ:D_CQ + D_CKV], gckv_ref[...])
    ckv_ref[...] = ckvn
    dq = proj[:, 512:1024]
    dv = proj[:, 1024:1536]
    dq_ref[...] = (dq * SCALE_B).astype(BF16)
    dvb_ref[...] = dv.astype(BF16)
    for h in range(H_B):
        dv4_ref[pl.ds(h, tm, stride=H_B), :] = dv[:, h * LANES:(h + 1) * LANES]
    dkt = projt[0:W_B]
    dkt_ref[0] = dkt
    dktb_ref[0] = dkt.astype(BF16)
    krt = projt[W_B:W_B + D_ROPE] * ctt_ref[...] + projt[W_B + D_ROPE:W_B + 2 * D_ROPE] * stt_ref[...]
    krt_ref[0] = krt
    krtb_ref[0] = krt.astype(BF16)
    ckvb = ckvn.astype(BF16)
    knt_ref[0] = _dot_nt(wukt_ref[...], ckvb).astype(BF16)
    va_ref[...] = _dot(ckvb, wv_ref[...]).astype(BF16)
    _rope_q(_dot(cqn.astype(BF16), wq_ref[...]), ct_ref[...], st_ref[...], qa_ref)


def _pre_t_call(x2d, b, s, tabs, weights, tm):
    t = x2d.shape[0]
    nt = s // tm
    row = lambda c: pl.BlockSpec((tm, c), lambda i, j: (i * nt + j, 0))
    tab = pl.BlockSpec((tm, LANES), lambda i, j: (j, 0))
    tabt = pl.BlockSpec((D_ROPE, tm), lambda i, j: (0, j))
    tr = lambda r: pl.BlockSpec((1, r, tm), lambda i, j: (i, 0, j))
    out_shapes = (
        jax.ShapeDtypeStruct((t, D_CKV), F32), jax.ShapeDtypeStruct((b, D_ROPE, s), F32),
        jax.ShapeDtypeStruct((b, W_B, s), F32), jax.ShapeDtypeStruct((t * H_B, LANES), F32),
        jax.ShapeDtypeStruct((t, H_A * LANES), BF16), jax.ShapeDtypeStruct((b, H_A * D_NOPE, s), BF16),
        jax.ShapeDtypeStruct((b, D_ROPE, s), BF16), jax.ShapeDtypeStruct((t, H_A * D_VA), BF16),
        jax.ShapeDtypeStruct((t, W_B), BF16), jax.ShapeDtypeStruct((b, W_B, s), BF16),
        jax.ShapeDtypeStruct((t, W_B), BF16))
    dv4_spec = pl.BlockSpec((tm * H_B, LANES), lambda i, j: (i * nt + j, 0))
    return pl.pallas_call(
        _pre_t_kernel,
        grid=(b, nt),
        in_specs=[row(D_MODEL), tab, tab, tabt, tabt] + [_const_spec(w.shape) for w in weights],
        out_specs=(row(D_CKV), tr(D_ROPE), tr(W_B), dv4_spec, row(H_A * LANES), tr(H_A * D_NOPE),
                   tr(D_ROPE), row(H_A * D_VA), row(W_B), tr(W_B), row(W_B)),
        out_shape=out_shapes,
        compiler_params=pltpu.CompilerParams(dimension_semantics=("arbitrary", "arbitrary"),
                                             vmem_limit_bytes=VMEM_LIMIT),
        name="pre_proj_prompt",
    )(x2d, *tabs, *weights)


def _pre_s_kernel(x_ref, ct_ref, st_ref, w1_ref, wq_ref, gcq_ref, gckv_ref,
                  ckv_ref, krope_ref, ndk_ref, ndv_ref, qa_ref, dq_ref, dk_ref, dv_ref):
    proj = _dot(x_ref[...].astype(BF16), w1_ref[...])
    cqn = _rms(proj[:, 0:D_CQ], gcq_ref[...])
    ckv_ref[...] = _rms(proj[:, D_CQ:D_CQ + D_CKV], gckv_ref[...])
    dq = proj[:, 512:1024]
    dk = proj[:, 1024:1536]
    dv = proj[:, 1536:2048]
    ndk_ref[...] = dk
    ndv_ref[...] = dv
    dq_ref[...] = (dq * SCALE_B).astype(BF16)
    dk_ref[...] = dk.astype(BF16)
    dv_ref[...] = dv.astype(BF16)
    ct = ct_ref[...]
    st = st_ref[...]
    krp = proj[:, 2048:2048 + LANES] * ct + proj[:, 2048 + LANES:2048 + 2 * LANES] * st
    krope_ref[...] = pltpu.roll(krp, 64, 1)[:, 0:D_ROPE]
    _rope_q(_dot(cqn.astype(BF16), wq_ref[...]), ct, st, qa_ref)


def _pre_s_call(x2d, ct, st, w1, wq, gcq, gckv, tm):
    t = x2d.shape[0]
    row = lambda c: pl.BlockSpec((tm, c), lambda i: (i, 0))
    tab = pl.BlockSpec((tm, LANES), lambda i: (0, 0))
    out_shapes = (
        jax.ShapeDtypeStruct((t, D_CKV), F32), jax.ShapeDtypeStruct((t, D_ROPE), F32),
        jax.ShapeDtypeStruct((t, W_B), F32), jax.ShapeDtypeStruct((t, W_B), F32),
        jax.ShapeDtypeStruct((t, H_A * LANES), BF16), jax.ShapeDtypeStruct((t, W_B), BF16),
        jax.ShapeDtypeStruct((t, W_B), BF16), jax.ShapeDtypeStruct((t, W_B), BF16))
    return pl.pallas_call(
        _pre_s_kernel,
        grid=(t // tm,),
        in_specs=[row(D_MODEL), tab, tab, _const_spec(w1.shape), _const_spec(wq.shape),
                  _const_spec(gcq.shape), _const_spec(gckv.shape)],
        out_specs=(row(D_CKV), row(D_ROPE), row(W_B), row(W_B), row(H_A * LANES), row(W_B), row(W_B), row(W_B)),
        out_shape=out_shapes,
        compiler_params=pltpu.CompilerParams(dimension_semantics=("arbitrary",),
                                             vmem_limit_bytes=VMEM_LIMIT),
        name="pre_proj_sample",
    )(x2d, ct, st, w1, wq, gcq, gckv)


def _chunk_mask():
    r = lax.broadcasted_iota(jnp.int32, (HQ, HQ), 0) // CHUNK
    c = lax.broadcasted_iota(jnp.int32, (HQ, HQ), 1) // CHUNK
    return c <= r


def _rowmax(x):
    return jnp.max(x, -1, keepdims=True)


def _rowsum(x):
    return jnp.sum(x, -1, keepdims=True)


def _tile_attend(q, kt_ref, mi, v_ref, r0, mask, corr):
    qt, qb = q[0:HQ], q[HQ:TQ]
    sdt = _dot(qt, kt_ref[mi, :, r0:r0 + HQ])
    sbb = _dot(qb, kt_ref[mi, :, r0:r0 + TQ])
    sbl, sbr = sbb[:, 0:HQ], sbb[:, HQ:TQ]
    if corr is not None:
        sdt = sdt - corr
        sbr = sbr - corr
    sdt = jnp.where(mask, sdt, -jnp.inf)
    sbr = jnp.where(mask, sbr, -jnp.inf)
    mt = _rowmax(sdt)
    mb = jnp.maximum(_rowmax(sbl), _rowmax(sbr))
    if r0 > 0:
        so = _dot(q, kt_ref[mi, :, 0:r0])
        mt = jnp.maximum(mt, _rowmax(so[0:HQ]))
        mb = jnp.maximum(mb, _rowmax(so[HQ:TQ]))
        m = jnp.concatenate([mt, mb], axis=0)
        po = jnp.exp2(so - m)
        l = _rowsum(po)
        o = _dot(po.astype(BF16), v_ref[0, 0:r0, :])
    pt = jnp.exp2(sdt - mt)
    pbl = jnp.exp2(sbl - mb)
    pbr = jnp.exp2(sbr - mb)
    ld = jnp.concatenate([_rowsum(pt), _rowsum(pbl) + _rowsum(pbr)], axis=0)
    pb = jnp.concatenate([pbl, pbr], axis=1).astype(BF16)
    od = jnp.concatenate([_dot(pt.astype(BF16), v_ref[0, r0:r0 + HQ, :]),
                          _dot(pb, v_ref[0, r0:r0 + TQ, :])], axis=0)
    if r0 > 0:
        return o + od, l + ld
    return od, ld


def _attn_a_kernel(q1_ref, q2_ref, knt_ref, krt_ref, v_ref, o_ref, kt_ref):
    s = q1_ref.shape[1]
    zpad = jnp.zeros((LANES - D_NOPE - D_ROPE, s), BF16)
    for i in range(2):
        kt_ref[i, 0:D_NOPE, :] = knt_ref[0, i * D_NOPE:(i + 1) * D_NOPE, :]
        kt_ref[i, D_NOPE:D_NOPE + D_ROPE, :] = krt_ref[0]
        kt_ref[i, D_NOPE + D_ROPE:LANES, :] = zpad
    mask = _chunk_mask()
    lane = lax.broadcasted_iota(jnp.int32, (TQ, LANES), 1)
    for qi in range(s // TQ):
        r0 = qi * TQ
        o1, l1 = _tile_attend(q1_ref[0, r0:r0 + TQ, :], kt_ref, 0, v_ref, r0, mask, None)
        o2, l2 = _tile_attend(q2_ref[0, r0:r0 + TQ, :], kt_ref, 1, v_ref, r0, mask, None)
        o_ref[0, r0:r0 + TQ, :] = jnp.where(lane < D_VA, o1 / l1, o2 / l2).astype(BF16)


def _lambda_value(lq1_ref, lk1_ref, lq2_ref, lk2_ref):
    a = jnp.sum(lq1_ref[...] * lk1_ref[...], -1, keepdims=True)
    b = jnp.sum(lq2_ref[...] * lk2_ref[...], -1, keepdims=True)
    return jnp.exp(a) - jnp.exp(b) + LAM_INIT


def _attn_b_kernel(slope_ref, qaug_ref, kaug_ref, lq1_ref, lk1_ref, lq2_ref, lk2_ref, g_ref,
                   q_ref, dkt_ref, v_ref, o_ref, kt_ref):
    s = q_ref.shape[1]
    kt_ref[0, 0:D_HB, :] = dkt_ref[0, 0:D_HB, :]
    kt_ref[0, D_HB:LANES, :] = kaug_ref[...]
    kt_ref[1, 0:D_HB, :] = kaug_ref[...]
    kt_ref[1, D_HB:LANES, :] = dkt_ref[0, D_HB:LANES, :]
    mask = _chunk_mask()
    lane = lax.broadcasted_iota(jnp.int32, (TQ, LANES), 1)
    slope2 = slope_ref[0, 0:1, 0:1] * (2.0 * LOG2E)
    lam = _lambda_value(lq1_ref, lk1_ref, lq2_ref, lk2_ref)
    ri = lax.broadcasted_iota(jnp.int32, (HQ, HQ), 0)
    ci = lax.broadcasted_iota(jnp.int32, (HQ, HQ), 1)
    corr = slope2 * jnp.maximum(ci - ri, 0).astype(F32)
    aug1 = qaug_ref[0, 0, 0:1, :].astype(BF16)
    aug2 = qaug_ref[0, 1, 0:1, :].astype(BF16)
    for qi in range(s // TQ):
        r0 = qi * TQ
        q = q_ref[0, r0:r0 + TQ, :]
        q1 = jnp.where(lane < D_HB, q, aug1)
        q2 = jnp.where(lane >= D_HB, q, aug2)
        o1, l1 = _tile_attend(q1, kt_ref, 0, v_ref, r0, mask, corr)
        o2, l2 = _tile_attend(q2, kt_ref, 1, v_ref, r0, mask, corr)
        o = o1 / l1 - lam * (o2 / l2)
        o_ref[0, r0:r0 + TQ, :] = (_rms(o, g_ref[...]) * (1.0 - LAM_INIT)).astype(BF16)


def _attn_a_call(qa, knt, krtb, va):
    b, s, _ = qa.shape
    blk = lambda f: pl.BlockSpec((1, s, LANES), f)
    return pl.pallas_call(
        _attn_a_kernel,
        grid=(b, H_A // 2),
        in_specs=[blk(lambda i, j: (i, 0, 2 * j)), blk(lambda i, j: (i, 0, 2 * j + 1)),
                  pl.BlockSpec((1, 2 * D_NOPE, s), lambda i, j: (i, j, 0)),
                  pl.BlockSpec((1, D_ROPE, s), lambda i, j: (i, 0, 0)),
                  blk(lambda i, j: (i, 0, j))],
        out_specs=blk(lambda i, j: (i, 0, j)),
        out_shape=jax.ShapeDtypeStruct((b, s, H_A * D_VA), BF16),
        scratch_shapes=[pltpu.VMEM((2, LANES, s), BF16)],
        compiler_params=pltpu.CompilerParams(dimension_semantics=("arbitrary", "arbitrary"),
                                             vmem_limit_bytes=VMEM_LIMIT),
        name="attn_mla",
    )(qa, qa, knt, krtb, va)


def _attn_b_call(slopes, qaug, kaug, lq1, lk1, lq2, lk2, g, dq, dktb, dvb):
    b, s, _ = dq.shape
    blk = pl.BlockSpec((1, s, LANES), lambda i, j: (i, 0, j))
    vec = lambda a: pl.BlockSpec(a.shape, lambda i, j: (0,) * a.ndim)
    return pl.pallas_call(
        _attn_b_kernel,
        grid=(b, H_B),
        in_specs=[pl.BlockSpec((1, 8, LANES), lambda i, j: (j, 0, 0)),
                  pl.BlockSpec((1, 2, 8, LANES), lambda i, j: (j, 0, 0, 0)),
                  vec(kaug), vec(lq1), vec(lk1), vec(lq2), vec(lk2), vec(g),
                  blk, pl.BlockSpec((1, LANES, s), lambda i, j: (i, j, 0)), blk],
        out_specs=blk,
        out_shape=jax.ShapeDtypeStruct((b, s, W_B), BF16),
        scratch_shapes=[pltpu.VMEM((2, LANES, s), BF16)],
        compiler_params=pltpu.CompilerParams(dimension_semantics=("arbitrary", "arbitrary"),
                                             vmem_limit_bytes=VMEM_LIMIT),
        name="attn_diff",
    )(slopes, qaug, kaug, lq1, lk1, lq2, lk2, g, dq, dktb, dvb)


def _stack_rows(x, reps, width):
    n, c = x.shape
    t = jnp.concatenate([x] * reps, axis=0)
    rb = lax.broadcasted_iota(jnp.int32, (reps * n, c), 0) // n
    cb = lax.broadcasted_iota(jnp.int32, (reps * n, c), 1) // width
    return jnp.where(rb == cb, t, jnp.zeros_like(t))


def _softmax2(sc, sn):
    m = jnp.maximum(_rowmax(sc), _rowmax(sn))
    pc = jnp.exp2(sc - m)
    pn = jnp.exp2(sn - m)
    l = _rowsum(pc) + _rowsum(pn)
    return pc.astype(BF16), pn.astype(BF16), l


def _attn_s_kernel(past_len, qa_ref, dq_ref, cckv_ref, ckrt_ref, cdkt_ref, cdv4_ref,
                   nckv_ref, nkr_ref, ndk_ref, ndv_ref, wabs_ref, wuvb_ref,
                   lq1_ref, lk1_ref, lq2_ref, lk2_ref, g_ref, oa_ref, ob_ref):
    n = qa_ref.shape[1]
    qblk = _stack_rows(qa_ref[0], H_A, LANES)
    g = _dot(qblk, wabs_ref[...])
    qabs = g[:, 0:D_CKV].astype(BF16)
    qrp = g[:, D_CKV:D_CKV + D_ROPE].astype(BF16)
    cc = cckv_ref[0].astype(BF16)
    cn = nckv_ref[0].astype(BF16)
    sc = _dot_nt(qabs, cc) + _dot(qrp, ckrt_ref[0].astype(BF16))
    sn = _dot_nt(qabs, cn) + _dot_nt(qrp, nkr_ref[0].astype(BF16))
    pc, pn, l = _softmax2(sc, sn)
    ctx = ((_dot(pc, cc) + _dot(pn, cn)) / l).astype(BF16)
    oa = _dot(ctx[0:n], wuvb_ref[0:D_CKV, :])
    for h in range(1, H_A):
        oa = oa + _dot(ctx[h * n:(h + 1) * n], wuvb_ref[h * D_CKV:(h + 1) * D_CKV, :])
    oa_ref[0] = oa.astype(BF16)
    rows = 2 * H_B * n
    qb = _stack_rows(dq_ref[0], 2 * H_B, D_HB)
    sc = _dot(qb, cdkt_ref[0].astype(BF16))
    sn = _dot_nt(qb, ndk_ref[0])
    r = lax.broadcasted_iota(jnp.int32, (rows, 1), 0)
    hh = r // (2 * n)
    slope = jnp.where(hh == 0, 2.0 ** -2, jnp.where(hh == 1, 2.0 ** -4,
                      jnp.where(hh == 2, 2.0 ** -6, 2.0 ** -8))).astype(F32) * LOG2E
    qpos = (r % n).astype(F32)
    kc = lax.broadcasted_iota(jnp.int32, (1, past_len), 1).astype(F32)
    kn = lax.broadcasted_iota(jnp.int32, (1, n), 1).astype(F32)
    sc = sc - slope * jnp.abs((qpos + float(past_len)) - kc)
    sn = sn - slope * jnp.abs(qpos - kn)
    pc, pn, l = _softmax2(sc, sn)
    lam = _lambda_value(lq1_ref, lk1_ref, lq2_ref, lk2_ref)
    inv_l = 1.0 / l
    for h in range(H_B):
        sl = slice(h * LANES, (h + 1) * LANES)
        rs = slice(2 * h * n, (2 * h + 2) * n)
        vh = cdv4_ref[0, pl.ds(h, past_len, stride=H_B), :].astype(BF16)
        of = (_dot(pc[rs], vh) + _dot(pn[rs], ndv_ref[0, :, sl])) * inv_l[rs]
        o = of[0:n] - lam * of[n:2 * n]
        ob_ref[0, :, sl] = (_rms(o, g_ref[...]) * (1.0 - LAM_INIT)).astype(BF16)


def _attn_s_call(qa, dq, cckv, ckrt, cdkt, cdv4, nckv, nkr, ndk, ndv, wabs, wuvb, lq1, lk1, lq2, lk2, g):
    b, n, _ = qa.shape
    past_len = cckv.shape[1]
    per_b = lambda a: pl.BlockSpec((1,) + a.shape[1:], lambda i: (i, 0, 0))
    vec = lambda a: pl.BlockSpec(a.shape, lambda i: (0,) * a.ndim)
    args = (qa, dq, cckv, ckrt, cdkt, cdv4, nckv, nkr, ndk, ndv)
    consts = (wabs, wuvb)
    vecs = (lq1, lk1, lq2, lk2, g)
    return pl.pallas_call(
        functools.partial(_attn_s_kernel, past_len),
        grid=(b,),
        in_specs=[per_b(a) for a in args] + [_const_spec(a.shape) for a in consts] + [vec(a) for a in vecs],
        out_specs=(pl.BlockSpec((1, n, H_A * D_VA), lambda i: (i, 0, 0)),
                   pl.BlockSpec((1, n, W_B), lambda i: (i, 0, 0))),
        out_shape=(jax.ShapeDtypeStruct((b, n, H_A * D_VA), BF16), jax.ShapeDtypeStruct((b, n, W_B), BF16)),
        compiler_params=pltpu.CompilerParams(dimension_semantics=("arbitrary",),
                                             vmem_limit_bytes=VMEM_LIMIT),
        name="attn_sample",
    )(*args, *consts, *vecs)


def _post_kernel(nb, tiles_per_seq, x_ref, oa_ref, ob_ref, p_ref, state_ref,
                 woa_ref, wob_ref, ln1g_ref, ln1b_ref, wup_ref, cw_ref, cb_ref, wdn_ref,
                 ln2g_ref, ln2b_ref, wg_ref, bg_ref, wp_ref,
                 y_ref, nconv_ref, carry_ref, upad_ref, act_ref):
    tm = x_ref.shape[0]
    s = tm // nb
    i = pl.program_id(0)

    @pl.when(i % tiles_per_seq == 0)
    def _():
        carry_ref[...] = state_ref[...]

    a = _dot(oa_ref[...], woa_ref[...]) + _dot(ob_ref[...], wob_ref[...])
    h = _ln(ALPHA * x_ref[...] + a, ln1g_ref[...], ln1b_ref[...])
    hb = h.astype(BF16)

    def conv(c0):
        u = _dot(hb, wup_ref[:, c0:c0 + FC])
        u3 = u.reshape(nb, s, FC)
        upad_ref[:, 6:8, :] = carry_ref[:, :, c0:c0 + FC]
        upad_ref[:, 8:8 + s, :] = u3
        tail = u3[:, s - 2:s, :]
        carry_ref[:, :, c0:c0 + FC] = tail
        nconv_ref[:, :, c0:c0 + FC] = tail
        z = (cb_ref[:, c0:c0 + FC]
             + upad_ref[:, 6:6 + s, :] * cw_ref[0:1, c0:c0 + FC]
             + upad_ref[:, 7:7 + s, :] * cw_ref[1:2, c0:c0 + FC]
             + u3 * cw_ref[2:3, c0:c0 + FC])
        return z.reshape(tm, FC)

    for c in range(N_FC):
        zg = conv(c * FC)
        zv = conv(D_FF + c * FC)
        act_ref[:, c * FC:(c + 1) * FC] = (_gelu(zg) * zv).astype(BF16)

    f = _dot(act_ref[...], wdn_ref[...])
    h2 = _ln(ALPHA * h + f, ln2g_ref[...], ln2b_ref[...])
    gate = jax.nn.sigmoid(_dot(h2.astype(BF16), wg_ref[...]) + bg_ref[...])
    y_ref[...] = h2 + gate * _dot(p_ref[...].astype(BF16), wp_ref[...])


def _post_call(x2d, oa, ob, p2d, state, weights, tm, nb, tiles_per_seq):
    t = x2d.shape[0]
    n_seq = state.shape[0]
    s = tm // nb
    row = lambda c: pl.BlockSpec((tm, c), lambda i: (i, 0))
    st_spec = pl.BlockSpec((nb, CONV_W - 1, 2 * D_FF), lambda i: (i // tiles_per_seq, 0, 0))
    return pl.pallas_call(
        functools.partial(_post_kernel, nb, tiles_per_seq),
        grid=(t // tm,),
        in_specs=[row(D_MODEL), row(H_A * D_VA), row(W_B), row(D_PLE), st_spec]
                 + [_const_spec(w.shape) for w in weights],
        out_specs=(row(D_MODEL), st_spec),
        out_shape=(jax.ShapeDtypeStruct((t, D_MODEL), F32),
                   jax.ShapeDtypeStruct((n_seq, CONV_W - 1, 2 * D_FF), F32)),
        scratch_shapes=[pltpu.VMEM((nb, CONV_W - 1, 2 * D_FF), F32),
                        pltpu.VMEM((nb, s + 8, FC), F32),
                        pltpu.VMEM((tm, D_FF), BF16)],
        compiler_params=pltpu.CompilerParams(dimension_semantics=("arbitrary",),
                                             vmem_limit_bytes=VMEM_LIMIT),
        name="post_ffn",
    )(x2d, oa, ob, p2d, state, *weights)


def _rope_angles(pos):
    half = D_ROPE // 2
    inv = 1.0 / (ROPE_BASE ** (jnp.arange(half, dtype=F32) / half))
    ang = pos.astype(F32)[:, None] * inv[None, :]
    return jnp.cos(ang), jnp.sin(ang)


def _rope_tables(pos):
    cos, sin = _rope_angles(pos)
    n = pos.shape[0]
    ct = jnp.concatenate([jnp.ones((n, D_NOPE), F32), cos, cos, jnp.zeros((n, LANES - D_NOPE - D_ROPE), F32)], 1)
    st = jnp.concatenate([jnp.zeros((n, D_NOPE), F32), sin, sin, jnp.zeros((n, LANES - D_NOPE - D_ROPE), F32)], 1)
    return ct, st


def _rope_tables_t(pos):
    cos, sin = _rope_angles(pos)
    return jnp.concatenate([cos, cos], 1).T, jnp.concatenate([sin, sin], 1).T


def _rot_half_cols(w):
    half = D_ROPE // 2
    return jnp.concatenate([-w[..., half:], w[..., :half]], -1)


def _prep_weights(w_in, w_uq, w_uk, w_uv):
    o1, o2, o3 = D_CQ, D_CQ + D_CKV, D_CQ + D_CKV + D_ROPE
    o4, o5 = o3 + W_B, o3 + 2 * W_B
    w_kr = w_in[:, o2:o3]
    w_dq, w_dk, w_dv = w_in[:, o3:o4], w_in[:, o4:o5], w_in[:, o5:]
    z = lambda c: jnp.zeros((D_MODEL, c), F32)
    w1p = jnp.concatenate([w_in[:, :o2], w_dq, w_dv], 1).astype(BF16)
    wtp = jnp.concatenate([w_dk, w_kr, _rot_half_cols(w_kr)], 1).T.astype(BF16)
    w1s = jnp.concatenate([w_in[:, :o2], w_dq, w_dk, w_dv,
                           z(D_NOPE), w_kr, z(LANES - D_NOPE - D_ROPE),
                           z(D_NOPE), _rot_half_cols(w_kr), z(LANES - D_NOPE - D_ROPE)], 1).astype(BF16)
    wq3 = w_uq.reshape(D_CQ, H_A, D_NOPE + D_ROPE)
    zq = lambda c: jnp.zeros((D_CQ, H_A, c), F32)
    wq_a = jnp.concatenate([wq3, zq(LANES - D_NOPE - D_ROPE)], -1).reshape(D_CQ, H_A * LANES)
    wq_b = jnp.concatenate([zq(D_NOPE), _rot_half_cols(wq3[..., D_NOPE:]), zq(LANES - D_NOPE - D_ROPE)],
                           -1).reshape(D_CQ, H_A * LANES)
    wq = jnp.concatenate([wq_a, wq_b], 1).astype(BF16)
    wk3 = w_uk.reshape(D_CKV, H_A, D_NOPE)
    wabs_lat = jnp.concatenate([jnp.transpose(wk3, (1, 2, 0)),
                                jnp.zeros((H_A, LANES - D_NOPE, D_CKV), F32)], 1)
    pick = jnp.zeros((LANES, LANES), F32).at[D_NOPE + jnp.arange(D_ROPE), jnp.arange(D_ROPE)].set(1.0)
    wabs = jnp.concatenate([wabs_lat, jnp.broadcast_to(pick, (H_A, LANES, LANES))], -1)
    wabs = wabs.reshape(H_A * LANES, D_CKV + LANES).astype(BF16)
    wv3 = w_uv.reshape(D_CKV, H_A, D_VA)
    eye = jnp.eye(H_A, dtype=F32)
    wuvb = jnp.einsum('lhd,hg->hlgd', wv3, eye).reshape(H_A * D_CKV, H_A * D_VA).astype(BF16)
    return w1p, wtp, w1s, wq, w_uk.T.astype(BF16), w_uv.astype(BF16), wabs, wuvb


def _alibi_tables(s):
    j = jnp.arange(s, dtype=jnp.int32)
    j_hi = ((j // 256) * 256).astype(F32)
    j_lo = (j % 256).astype(F32)
    kaug = jnp.zeros((D_HB, s), F32).at[0:N_AUG].set(jnp.stack([j_hi, j_lo] * 3)).astype(BF16)
    c_hi = jnp.asarray(LOG2E, F32).astype(BF16).astype(F32)
    c_mid = (jnp.asarray(LOG2E, F32) - c_hi).astype(BF16).astype(F32)
    c_lo = (jnp.asarray(LOG2E, F32) - c_hi - c_mid).astype(BF16).astype(F32)
    consts = jnp.stack([c_hi, c_hi, c_mid, c_mid, c_lo, c_lo])
    slopes = jnp.asarray([2.0 ** (-8.0 * (h + 1) / H_B) for h in range(H_B)], F32)
    vals = slopes[:, None] * consts[None, :]
    qaug = jnp.zeros((H_B, 2, 8, LANES), F32)
    qaug = qaug.at[:, 0, :, D_HB:D_HB + N_AUG].set(jnp.broadcast_to(vals[:, None, :], (H_B, 8, N_AUG)))
    qaug = qaug.at[:, 1, :, 0:N_AUG].set(jnp.broadcast_to(vals[:, None, :], (H_B, 8, N_AUG)))
    return kaug, qaug, jnp.broadcast_to(slopes[:, None, None], (H_B, 8, LANES))


def kernel(x_prompt, x_sample, cache_ckv, cache_krope, cache_diff_k, cache_diff_v, state_ffn_conv, p_prompt, p_sample, w_in, g_cq, w_uq, g_ckv, w_uk, w_uv, lambda_q1, lambda_k1, lambda_q2, lambda_k2, g_subln, w_o, ln1_g, ln1_b, w_up, conv_w, conv_b, w_down, ln2_g, ln2_b, w_ple_gate, b_ple_gate, w_ple_proj):
    b, s, _ = x_prompt.shape
    bs, n, _ = x_sample.shape
    past_len = cache_ckv.shape[2]
    w1p, wtp, w1s, wq, wukt, wv, wabs, wuvb = _prep_weights(w_in[0], w_uq[0], w_uk[0], w_uv[0])
    post_w = (w_o[0][:H_A * D_VA].astype(BF16), w_o[0][H_A * D_VA:].astype(BF16), ln1_g, ln1_b,
              w_up[0].astype(BF16), conv_w[0], conv_b, w_down[0].astype(BF16), ln2_g, ln2_b,
              w_ple_gate[0].astype(BF16), b_ple_gate, w_ple_proj[0].astype(BF16))
    kaug, qaug, slopes = _alibi_tables(s)
    lams = (lambda_q1, lambda_k1, lambda_q2, lambda_k2)

    tm = 512
    pos = jnp.arange(s)
    tabs = _rope_tables(pos) + _rope_tables_t(pos)
    x2p = x_prompt.reshape(b * s, D_MODEL)
    (ckv_p, krt_p, dkt_p, dv4_p, qa, knt, krtb, va, dq, dktb, dvb) = _pre_t_call(
        x2p, b, s, tabs, (w1p, wtp, wq, wukt, wv, g_cq, g_ckv), tm)
    r3 = lambda a: a.reshape(b, s, a.shape[-1])
    oa = _attn_a_call(r3(qa), knt, krtb, r3(va))
    ob = _attn_b_call(slopes, qaug, kaug, *lams, g_subln, r3(dq), dktb, r3(dvb))
    zero_state = jnp.zeros((b, CONV_W - 1, 2 * D_FF), F32)
    y_p, conv_p = _post_call(x2p, oa.reshape(b * s, -1), ob.reshape(b * s, -1),
                             p_prompt[0].reshape(b * s, D_PLE), zero_state, post_w, tm, 1, s // tm)

    nbs = 8
    tms = nbs * n
    cts, sts = _rope_tables(past_len + jnp.arange(n))
    cts, sts = jnp.tile(cts, (nbs, 1)), jnp.tile(sts, (nbs, 1))
    x2s = x_sample.reshape(bs * n, D_MODEL)
    (ckv_s, kr_s, ndk_s, ndv_s, qa_s, dq_s, dk_s, dv_s) = _pre_s_call(x2s, cts, sts, w1s, wq, g_cq, g_ckv, tms)
    q3 = lambda a: a.reshape(bs, n, a.shape[-1])
    ckrt = jnp.transpose(cache_krope[0], (0, 2, 1))
    cdkt = jnp.transpose(cache_diff_k[0], (0, 2, 3, 4, 1)).reshape(bs, W_B, past_len)
    cdv4 = cache_diff_v[0].reshape(bs, past_len * H_B, 2 * D_HB)
    oa_s, ob_s = _attn_s_call(
        q3(qa_s), q3(dq_s), cache_ckv[0], ckrt, cdkt, cdv4,
        q3(ckv_s), q3(kr_s), q3(dk_s), q3(dv_s), wabs, wuvb, *lams, g_subln)
    y_s, conv_s = _post_call(x2s, oa_s.reshape(bs * n, -1), ob_s.reshape(bs * n, -1),
                             p_sample[0].reshape(bs * n, D_PLE), state_ffn_conv[0], post_w, tms, nbs, 1)

    new_kr_p = jnp.transpose(krt_p, (0, 2, 1))[None]
    new_dk_p = jnp.transpose(dkt_p.reshape(b, H_B, 2, D_HB, s), (0, 4, 1, 2, 3))[None]
    return (y_p.reshape(b, s, D_MODEL), y_s.reshape(bs, n, D_MODEL),
            ckv_p.reshape(1, b, s, D_CKV), new_kr_p, new_dk_p,
            dv4_p.reshape(1, b, s, H_B, 2 * D_HB),
            conv_p[None],
            ckv_s.reshape(1, bs, n, D_CKV), kr_s.reshape(1, bs, n, D_ROPE),
            ndk_s.reshape(1, bs, n, H_B, 2, D_HB), ndv_s.reshape(1, bs, n, H_B, 2 * D_HB),
            conv_s[None])
```

```python
import functools
import math

import numpy as np
import jax
import jax.numpy as jnp
from jax import lax
from jax.experimental import pallas as pl
from jax.experimental.pallas import tpu as pltpu

F32 = jnp.float32
BF16 = jnp.bfloat16

D_MODEL = 1024
DEPTH = 1
CHUNK = 64
H_A = 8
D_NOPE = 64
D_ROPE = 32
D_VA = 64
D_CQ = 256
D_CKV = 256
ROPE_BASE = 10000.0
H_B = 4
D_HB = 64
W_B = H_B * 2 * D_HB
D_FF = 2816
CONV_W = 3
D_PLE = 256
ALPHA = (2 * DEPTH) ** 0.25
EPS = 1e-5
LAM_INIT = 0.8 - 0.6 * math.exp(-0.3 * 0)
LOG2E = 1.4426950408889634
SCALE_A = (D_NOPE + D_ROPE) ** -0.5 * LOG2E
SCALE_B = D_HB ** -0.5 * LOG2E

LANES = 128
VMEM_LIMIT = 56 * 1024 * 1024
TQ = 512
HQ = TQ // 2
FC = 256
N_FC = D_FF // FC
N_AUG = 6


def _dot(a, b):
    return jnp.dot(a, b, preferred_element_type=F32)


def _dot_nt(a, b):
    return lax.dot_general(a, b, (((1,), (1,)), ((), ())), preferred_element_type=F32)


def _const_spec(shape):
    nd = len(shape)
    return pl.BlockSpec(shape, lambda *_: (0,) * nd, pipeline_mode=pl.Buffered(1))


def _rms(v, g):
    return v * lax.rsqrt(jnp.mean(v * v, -1, keepdims=True) + EPS) * g


def _gelu(v):
    return 0.5 * v * (1.0 + lax.erf(v * (2.0 ** -0.5)))


def _ln(v, g, b):
    mu = jnp.mean(v, -1, keepdims=True)
    d = v - mu
    var = jnp.mean(d * d, -1, keepdims=True)
    return d * lax.rsqrt(var + EPS) * g + b


def _rope_q(qq, ct, st, qa_ref):
    for h in range(H_A):
        sl = slice(h * LANES, (h + 1) * LANES)
        qh = qq[:, sl] * ct + qq[:, H_A * LANES + h * LANES:H_A * LANES + (h + 1) * LANES] * st
        qa_ref[:, sl] = (qh * SCALE_A).astype(BF16)


def _pre_t_kernel(x_ref, ct_ref, st_ref, ctt_ref, stt_ref, w1_ref, wt_ref, wq_ref, wukt_ref, wv_ref,
                  gcq_ref, gckv_ref,
                  ckv_ref, krt_ref, dkt_ref, dv4_ref,
                  qa_ref, knt_ref, krtb_ref, va_ref, dq_ref, dktb_ref, dvb_ref):
    tm = x_ref.shape[0]
    xb = x_ref[...].astype(BF16)
    proj = _dot(xb, w1_ref[...])
    projt = _dot_nt(wt_ref[...], xb)
    cqn = _rms(proj[:, 0:D_CQ], gcq_ref[...])
    ckvn = _rms(proj[:, D_CQ:D_CQ + D_CKV], gckv_ref[...])
    ckv_ref[...] = ckvn
    dq = proj[:, 512:1024]
    dv = proj[:, 1024:1536]
    dq_ref[...] = (dq * SCALE_B).astype(BF16)
    dvb_ref[...] = dv.astype(BF16)
    for h in range(H_B):
        dv4_ref[pl.ds(h, tm, stride=H_B), :] = dv[:, h * LANES:(h + 1) * LANES]
    dkt = projt[0:W_B]
    dkt_ref[0] = dkt
    dktb_ref[0] = dkt.astype(BF16)
    krt = projt[W_B:W_B + D_ROPE] * ctt_ref[...] + projt[W_B + D_ROPE:W_B + 2 * D_ROPE] * stt_ref[...]
    krt_ref[0] = krt
    krtb_ref[0] = krt.astype(BF16)
    ckvb = ckvn.astype(BF16)
    knt_ref[0] = _dot_nt(wukt_ref[...], ckvb).astype(BF16)
    va_ref[...] = _dot(ckvb, wv_ref[...]).astype(BF16)
    _rope_q(_dot(cqn.astype(BF16), wq_ref[...]), ct_ref[...], st_ref[...], qa_ref)


def _pre_t_call(x2d, b, s, tabs, weights, tm):
    t = x2d.shape[0]
    nt = s // tm
    row = lambda c: pl.BlockSpec((tm, c), lambda i, j: (i * nt + j, 0))
    tab = pl.BlockSpec((tm, LANES), lambda i, j: (j, 0))
    tabt = pl.BlockSpec((D_ROPE, tm), lambda i, j: (0, j))
    tr = lambda r: pl.BlockSpec((1, r, tm), lambda i, j: (i, 0, j))
    out_shapes = (
        jax.ShapeDtypeStruct((t, D_CKV), F32), jax.ShapeDtypeStruct((b, D_ROPE, s), F32),
        jax.ShapeDtypeStruct((b, W_B, s), F32), jax.ShapeDtypeStruct((t * H_B, LANES), F32),
        jax.ShapeDtypeStruct((t, H_A * LANES), BF16), jax.ShapeDtypeStruct((b, H_A * D_NOPE, s), BF16),
        jax.ShapeDtypeStruct((b, D_ROPE, s), BF16), jax.ShapeDtypeStruct((t, H_A * D_VA), BF16),
        jax.ShapeDtypeStruct((t, W_B), BF16), jax.ShapeDtypeStruct((b, W_B, s), BF16),
        jax.ShapeDtypeStruct((t, W_B), BF16))
    dv4_spec = pl.BlockSpec((tm * H_B, LANES), lambda i, j: (i * nt + j, 0))
    return pl.pallas_call(
        _pre_t_kernel,
        grid=(b, nt),
        in_specs=[row(D_MODEL), tab, tab, tabt, tabt] + [_const_spec(w.shape) for w in weights],
        out_specs=(row(D_CKV), tr(D_ROPE), tr(W_B), dv4_spec, row(H_A * LANES), tr(H_A * D_NOPE),
                   tr(D_ROPE), row(H_A * D_VA), row(W_B), tr(W_B), row(W_B)),
        out_shape=out_shapes,
        compiler_params=pltpu.CompilerParams(dimension_semantics=("arbitrary", "arbitrary"),
                                             vmem_limit_bytes=VMEM_LIMIT),
        name="pre_proj_prompt",
    )(x2d, *tabs, *weights)


def _pre_s_kernel(x_ref, ct_ref, st_ref, w1_ref, wq_ref, gcq_ref, gckv_ref,
                  ckv_ref, krope_ref, ndk_ref, ndv_ref, qa_ref, dq_ref, dk_ref, dv_ref):
    proj = _dot(x_ref[...].astype(BF16), w1_ref[...])
    cqn = _rms(proj[:, 0:D_CQ], gcq_ref[...])
    ckv_ref[...] = _rms(proj[:, D_CQ:D_CQ + D_CKV], gckv_ref[...])
    dq = proj[:, 512:1024]
    dk = proj[:, 1024:1536]
    dv = proj[:, 1536:2048]
    ndk_ref[...] = dk
    ndv_ref[...] = dv
    dq_ref[...] = (dq * SCALE_B).astype(BF16)
    dk_ref[...] = dk.astype(BF16)
    dv_ref[...] = dv.astype(BF16)
    ct = ct_ref[...]
    st = st_ref[...]
    krp = proj[:, 2048:2048 + LANES] * ct + proj[:, 2048 + LANES:2048 + 2 * LANES] * st
    krope_ref[...] = pltpu.roll(krp, 64, 1)[:, 0:D_ROPE]
    _rope_q(_dot(cqn.astype(BF16), wq_ref[...]), ct, st, qa_ref)


def _pre_s_call(x2d, ct, st, w1, wq, gcq, gckv, tm):
    t = x2d.shape[0]
    row = lambda c: pl.BlockSpec((tm, c), lambda i: (i, 0))
    tab = pl.BlockSpec((tm, LANES), lambda i: (0, 0))
    out_shapes = (
        jax.ShapeDtypeStruct((t, D_CKV), F32), jax.ShapeDtypeStruct((t, D_ROPE), F32),
        jax.ShapeDtypeStruct((t, W_B), F32), jax.ShapeDtypeStruct((t, W_B), F32),
        jax.ShapeDtypeStruct((t, H_A * LANES), BF16), jax.ShapeDtypeStruct((t, W_B), BF16),
        jax.ShapeDtypeStruct((t, W_B), BF16), jax.ShapeDtypeStruct((t, W_B), BF16))
    return pl.pallas_call(
        _pre_s_kernel,
        grid=(t // tm,),
        in_specs=[row(D_MODEL), tab, tab, _const_spec(w1.shape), _const_spec(wq.shape),
                  _const_spec(gcq.shape), _const_spec(gckv.shape)],
        out_specs=(row(D_CKV), row(D_ROPE), row(W_B), row(W_B), row(H_A * LANES), row(W_B), row(W_B), row(W_B)),
        out_shape=out_shapes,
        compiler_params=pltpu.CompilerParams(dimension_semantics=("arbitrary",),
                                             vmem_limit_bytes=VMEM_LIMIT),
        name="pre_proj_sample",
    )(x2d, ct, st, w1, wq, gcq, gckv)


def _chunk_mask():
    r = lax.broadcasted_iota(jnp.int32, (HQ, HQ), 0) // CHUNK
    c = lax.broadcasted_iota(jnp.int32, (HQ, HQ), 1) // CHUNK
    return c <= r


def _rowmax(x):
    return jnp.max(x, -1, keepdims=True)


def _rowsum(x):
    return jnp.sum(x, -1, keepdims=True)


def _scores(q, kt_ref, mi, r0, mask, corr):
    so = _dot(q, kt_ref[mi, :, 0:r0]) if r0 > 0 else None
    sdt = _dot(q[0:HQ], kt_ref[mi, :, r0:r0 + HQ])
    sbb = _dot(q[HQ:TQ], kt_ref[mi, :, r0:r0 + TQ])
    sbl, sbr = sbb[:, 0:HQ], sbb[:, HQ:TQ]
    if corr is not None:
        sdt = sdt - corr
        sbr = sbr - corr
    return so, jnp.where(mask, sdt, -jnp.inf), sbl, jnp.where(mask, sbr, -jnp.inf)


def _probs(sc):
    so, sdt, sbl, sbr = sc
    mt = _rowmax(sdt)
    mb = jnp.maximum(_rowmax(sbl), _rowmax(sbr))
    if so is not None:
        mt = jnp.maximum(mt, _rowmax(so[0:HQ]))
        mb = jnp.maximum(mb, _rowmax(so[HQ:TQ]))
    pt = jnp.exp2(sdt - mt)
    pbl = jnp.exp2(sbl - mb)
    pbr = jnp.exp2(sbr - mb)
    l = jnp.concatenate([_rowsum(pt), _rowsum(pbl) + _rowsum(pbr)], axis=0)
    pb = jnp.concatenate([pbl, pbr], axis=1).astype(BF16)
    if so is None:
        return None, pt.astype(BF16), pb, l
    po = jnp.exp2(so - jnp.concatenate([mt, mb], axis=0))
    return po.astype(BF16), pt.astype(BF16), pb, l + _rowsum(po)


def _weighted_values(pr, v_ref, r0):
    po, pt, pb, l = pr
    o = jnp.concatenate([_dot(pt, v_ref[0, r0:r0 + HQ, :]), _dot(pb, v_ref[0, r0:r0 + TQ, :])], axis=0)
    if po is not None:
        o = o + _dot(po, v_ref[0, 0:r0, :])
    return o, l


def _attend_pipelined(n_tiles, get_q, kt_ref, v_ref, mask, corr, finish, ahead):
    chains = [(qi, mi) for qi in range(n_tiles) for mi in range(2)]
    sc, res = {}, {}

    def scores(c):
        qi, mi = chains[c]
        sc[c] = _scores(get_q(qi, mi), kt_ref, mi, qi * TQ, mask, corr)

    for c in range(min(ahead, len(chains))):
        scores(c)
    for c, (qi, mi) in enumerate(chains):
        pr = _probs(sc.pop(c))
        if c + ahead < len(chains):
            scores(c + ahead)
        res[c] = _weighted_values(pr, v_ref, qi * TQ)
        if mi == 1:
            finish(qi, res.pop(c - 1), res.pop(c))


def _attn_a_kernel(q1_ref, q2_ref, knt_ref, krt_ref, v_ref, o_ref, kt_ref):
    s = q1_ref.shape[1]
    zpad = jnp.zeros((LANES - D_NOPE - D_ROPE, s), BF16)
    for i in range(2):
        kt_ref[i, 0:D_NOPE, :] = knt_ref[0, i * D_NOPE:(i + 1) * D_NOPE, :]
        kt_ref[i, D_NOPE:D_NOPE + D_ROPE, :] = krt_ref[0]
        kt_ref[i, D_NOPE + D_ROPE:LANES, :] = zpad
    lane = lax.broadcasted_iota(jnp.int32, (TQ, LANES), 1)
    q_refs = (q1_ref, q2_ref)

    def get_q(qi, mi):
        return q_refs[mi][0, qi * TQ:(qi + 1) * TQ, :]

    def finish(qi, r1, r2):
        o_ref[0, qi * TQ:(qi + 1) * TQ, :] = jnp.where(lane < D_VA, r1[0] / r1[1], r2[0] / r2[1]).astype(BF16)

    _attend_pipelined(s // TQ, get_q, kt_ref, v_ref, _chunk_mask(), None, finish, ahead=2)


def _lambda_value(lq1_ref, lk1_ref, lq2_ref, lk2_ref):
    a = jnp.sum(lq1_ref[...] * lk1_ref[...], -1, keepdims=True)
    b = jnp.sum(lq2_ref[...] * lk2_ref[...], -1, keepdims=True)
    return jnp.exp(a) - jnp.exp(b) + LAM_INIT


def _attn_b_kernel(slope_ref, qaug_ref, kaug_ref, lq1_ref, lk1_ref, lq2_ref, lk2_ref, g_ref,
                   q_ref, dkt_ref, v_ref, o_ref, kt_ref):
    s = q_ref.shape[1]
    kt_ref[0, 0:D_HB, :] = dkt_ref[0, 0:D_HB, :]
    kt_ref[0, D_HB:LANES, :] = kaug_ref[...]
    kt_ref[1, 0:D_HB, :] = kaug_ref[...]
    kt_ref[1, D_HB:LANES, :] = dkt_ref[0, D_HB:LANES, :]
    mask = _chunk_mask()
    lane = lax.broadcasted_iota(jnp.int32, (TQ, LANES), 1)
    slope2 = slope_ref[0, 0:1, 0:1] * (2.0 * LOG2E)
    lam = _lambda_value(lq1_ref, lk1_ref, lq2_ref, lk2_ref)
    ri = lax.broadcasted_iota(jnp.int32, (HQ, HQ), 0)
    ci = lax.broadcasted_iota(jnp.int32, (HQ, HQ), 1)
    corr = slope2 * jnp.maximum(ci - ri, 0).astype(F32)
    aug1 = qaug_ref[0, 0, 0:1, :].astype(BF16)
    aug2 = qaug_ref[0, 1, 0:1, :].astype(BF16)

    def get_q(qi, mi):
        q = q_ref[0, qi * TQ:(qi + 1) * TQ, :]
        return jnp.where(lane < D_HB, q, aug1) if mi == 0 else jnp.where(lane >= D_HB, q, aug2)

    def finish(qi, r1, r2):
        o = r1[0] / r1[1] - lam * (r2[0] / r2[1])
        o_ref[0, qi * TQ:(qi + 1) * TQ, :] = (_rms(o, g_ref[...]) * (1.0 - LAM_INIT)).astype(BF16)

    _attend_pipelined(s // TQ, get_q, kt_ref, v_ref, mask, corr, finish, ahead=3)


def _attn_a_call(qa, knt, krtb, va):
    b, s, _ = qa.shape
    blk = lambda f: pl.BlockSpec((1, s, LANES), f)
    return pl.pallas_call(
        _attn_a_kernel,
        grid=(b, H_A // 2),
        in_specs=[blk(lambda i, j: (i, 0, 2 * j)), blk(lambda i, j: (i, 0, 2 * j + 1)),
                  pl.BlockSpec((1, 2 * D_NOPE, s), lambda i, j: (i, j, 0)),
                  pl.BlockSpec((1, D_ROPE, s), lambda i, j: (i, 0, 0)),
                  blk(lambda i, j: (i, 0, j))],
        out_specs=blk(lambda i, j: (i, 0, j)),
        out_shape=jax.ShapeDtypeStruct((b, s, H_A * D_VA), BF16),
        scratch_shapes=[pltpu.VMEM((2, LANES, s), BF16)],
        compiler_params=pltpu.CompilerParams(dimension_semantics=("arbitrary", "arbitrary"),
                                             vmem_limit_bytes=VMEM_LIMIT),
        name="attn_mla",
    )(qa, qa, knt, krtb, va)


def _attn_b_call(slopes, qaug, kaug, lq1, lk1, lq2, lk2, g, dq, dktb, dvb):
    b, s, _ = dq.shape
    blk = pl.BlockSpec((1, s, LANES), lambda i, j: (i, 0, j))
    vec = lambda a: pl.BlockSpec(a.shape, lambda i, j: (0,) * a.ndim)
    return pl.pallas_call(
        _attn_b_kernel,
        grid=(b, H_B),
        in_specs=[pl.BlockSpec((1, 8, LANES), lambda i, j: (j, 0, 0)),
                  pl.BlockSpec((1, 2, 8, LANES), lambda i, j: (j, 0, 0, 0)),
                  vec(kaug), vec(lq1), vec(lk1), vec(lq2), vec(lk2), vec(g),
                  blk, pl.BlockSpec((1, LANES, s), lambda i, j: (i, j, 0)), blk],
        out_specs=blk,
        out_shape=jax.ShapeDtypeStruct((b, s, W_B), BF16),
        scratch_shapes=[pltpu.VMEM((2, LANES, s), BF16)],
        compiler_params=pltpu.CompilerParams(dimension_semantics=("arbitrary", "arbitrary"),
                                             vmem_limit_bytes=VMEM_LIMIT),
        name="attn_diff",
    )(slopes, qaug, kaug, lq1, lk1, lq2, lk2, g, dq, dktb, dvb)


def _stack_rows(x, reps, width):
    n, c = x.shape
    t = jnp.concatenate([x] * reps, axis=0)
    rb = lax.broadcasted_iota(jnp.int32, (reps * n, c), 0) // n
    cb = lax.broadcasted_iota(jnp.int32, (reps * n, c), 1) // width
    return jnp.where(rb == cb, t, jnp.zeros_like(t))


def _softmax2(sc, sn):
    m = jnp.maximum(_rowmax(sc), _rowmax(sn))
    pc = jnp.exp2(sc - m)
    pn = jnp.exp2(sn - m)
    l = _rowsum(pc) + _rowsum(pn)
    return pc.astype(BF16), pn.astype(BF16), l


def _attn_s_kernel(past_len, qa_ref, dq_ref, cckv_ref, ckrt_ref, cdkt_ref, cdv4_ref,
                   nckv_ref, nkr_ref, ndk_ref, ndv_ref, wabs_ref, wuvb_ref,
                   lq1_ref, lk1_ref, lq2_ref, lk2_ref, g_ref, oa_ref, ob_ref):
    n = qa_ref.shape[1]
    qblk = _stack_rows(qa_ref[0], H_A, LANES)
    g = _dot(qblk, wabs_ref[...])
    qabs = g[:, 0:D_CKV].astype(BF16)
    qrp = g[:, D_CKV:D_CKV + D_ROPE].astype(BF16)
    cc = cckv_ref[0].astype(BF16)
    cn = nckv_ref[0].astype(BF16)
    sc = _dot_nt(qabs, cc) + _dot(qrp, ckrt_ref[0].astype(BF16))
    sn = _dot_nt(qabs, cn) + _dot_nt(qrp, nkr_ref[0].astype(BF16))
    pc, pn, l = _softmax2(sc, sn)
    ctx = ((_dot(pc, cc) + _dot(pn, cn)) / l).astype(BF16)
    oa = _dot(ctx[0:n], wuvb_ref[0:D_CKV, :])
    for h in range(1, H_A):
        oa = oa + _dot(ctx[h * n:(h + 1) * n], wuvb_ref[h * D_CKV:(h + 1) * D_CKV, :])
    oa_ref[0] = oa.astype(BF16)
    rows = 2 * H_B * n
    qb = _stack_rows(dq_ref[0], 2 * H_B, D_HB)
    sc = _dot(qb, cdkt_ref[0].astype(BF16))
    sn = _dot_nt(qb, ndk_ref[0])
    r = lax.broadcasted_iota(jnp.int32, (rows, 1), 0)
    hh = r // (2 * n)
    slope = jnp.where(hh == 0, 2.0 ** -2, jnp.where(hh == 1, 2.0 ** -4,
                      jnp.where(hh == 2, 2.0 ** -6, 2.0 ** -8))).astype(F32) * LOG2E
    qpos = (r % n).astype(F32)
    kc = lax.broadcasted_iota(jnp.int32, (1, past_len), 1).astype(F32)
    kn = lax.broadcasted_iota(jnp.int32, (1, n), 1).astype(F32)
    sc = sc - slope * jnp.abs((qpos + float(past_len)) - kc)
    sn = sn - slope * jnp.abs(qpos - kn)
    pc, pn, l = _softmax2(sc, sn)
    lam = _lambda_value(lq1_ref, lk1_ref, lq2_ref, lk2_ref)
    inv_l = 1.0 / l
    for h in range(H_B):
        sl = slice(h * LANES, (h + 1) * LANES)
        rs = slice(2 * h * n, (2 * h + 2) * n)
        vh = cdv4_ref[0, pl.ds(h, past_len, stride=H_B), :].astype(BF16)
        of = (_dot(pc[rs], vh) + _dot(pn[rs], ndv_ref[0, :, sl])) * inv_l[rs]
        o = of[0:n] - lam * of[n:2 * n]
        ob_ref[0, :, sl] = (_rms(o, g_ref[...]) * (1.0 - LAM_INIT)).astype(BF16)


def _attn_s_call(qa, dq, cckv, ckrt, cdkt, cdv4, nckv, nkr, ndk, ndv, wabs, wuvb, lq1, lk1, lq2, lk2, g):
    b, n, _ = qa.shape
    past_len = cckv.shape[1]
    per_b = lambda a: pl.BlockSpec((1,) + a.shape[1:], lambda i: (i, 0, 0))
    vec = lambda a: pl.BlockSpec(a.shape, lambda i: (0,) * a.ndim)
    args = (qa, dq, cckv, ckrt, cdkt, cdv4, nckv, nkr, ndk, ndv)
    consts = (wabs, wuvb)
    vecs = (lq1, lk1, lq2, lk2, g)
    return pl.pallas_call(
        functools.partial(_attn_s_kernel, past_len),
        grid=(b,),
        in_specs=[per_b(a) for a in args] + [_const_spec(a.shape) for a in consts] + [vec(a) for a in vecs],
        out_specs=(pl.BlockSpec((1, n, H_A * D_VA), lambda i: (i, 0, 0)),
                   pl.BlockSpec((1, n, W_B), lambda i: (i, 0, 0))),
        out_shape=(jax.ShapeDtypeStruct((b, n, H_A * D_VA), BF16), jax.ShapeDtypeStruct((b, n, W_B), BF16)),
        compiler_params=pltpu.CompilerParams(dimension_semantics=("arbitrary",),
                                             vmem_limit_bytes=VMEM_LIMIT),
        name="attn_sample",
    )(*args, *consts, *vecs)


def _post_kernel(nb, tiles_per_seq, x_ref, oa_ref, ob_ref, p_ref, state_ref,
                 woa_ref, wob_ref, ln1g_ref, ln1b_ref, wup_ref, cw_ref, cb_ref, wdn_ref,
                 ln2g_ref, ln2b_ref, wg_ref, bg_ref, wp_ref,
                 y_ref, nconv_ref, carry_ref, ush_ref, act_ref):
    tm = x_ref.shape[0]
    s = tm // nb
    i = pl.program_id(0)

    @pl.when(i % tiles_per_seq == 0)
    def _():
        carry_ref[...] = state_ref[...]

    a = _dot(oa_ref[...], woa_ref[...]) + _dot(ob_ref[...], wob_ref[...])
    h = _ln(ALPHA * x_ref[...] + a, ln1g_ref[...], ln1b_ref[...])
    hb = h.astype(BF16)

    def conv(c0):
        u = _dot(hb, wup_ref[:, c0:c0 + FC])
        u3 = u.reshape(nb, s, FC)
        tail = u3[:, s - 2:s, :]
        z = cb_ref[:, c0:c0 + FC] + u3 * cw_ref[2:3, c0:c0 + FC]
        zs = []
        for k in range(FC // LANES):
            cl = slice(c0 + k * LANES, c0 + (k + 1) * LANES)
            sl = slice(k * LANES, (k + 1) * LANES)
            for b in range(nb):
                ush_ref[k, b, pl.ds(2 * 6, 2, stride=2), :] = carry_ref[b, :, cl]
                ush_ref[k, b, pl.ds(2 * 8, s, stride=2), :] = u3[b, :, sl]
            um2 = jnp.stack([ush_ref[k, b, pl.ds(2 * 6, s, stride=2), :] for b in range(nb)])
            um1 = jnp.stack([ush_ref[k, b, pl.ds(2 * 7, s, stride=2), :] for b in range(nb)])
            zs.append(z[:, :, sl] + um2 * cw_ref[0:1, cl] + um1 * cw_ref[1:2, cl])
        carry_ref[:, :, c0:c0 + FC] = tail
        nconv_ref[:, :, c0:c0 + FC] = tail
        return jnp.concatenate(zs, axis=-1).reshape(tm, FC)

    for c in range(N_FC):
        zg = conv(c * FC)
        zv = conv(D_FF + c * FC)
        act_ref[:, c * FC:(c + 1) * FC] = (_gelu(zg) * zv).astype(BF16)

    f = _dot(act_ref[...], wdn_ref[...])
    h2 = _ln(ALPHA * h + f, ln2g_ref[...], ln2b_ref[...])
    gate = jax.nn.sigmoid(_dot(h2.astype(BF16), wg_ref[...]) + bg_ref[...])
    y_ref[...] = h2 + gate * _dot(p_ref[...].astype(BF16), wp_ref[...])


def _post_call(x2d, oa, ob, p2d, state, weights, tm, nb, tiles_per_seq):
    t = x2d.shape[0]
    n_seq = state.shape[0]
    s = tm // nb
    row = lambda c: pl.BlockSpec((tm, c), lambda i: (i, 0))
    st_spec = pl.BlockSpec((nb, CONV_W - 1, 2 * D_FF), lambda i: (i // tiles_per_seq, 0, 0))
    return pl.pallas_call(
        functools.partial(_post_kernel, nb, tiles_per_seq),
        grid=(t // tm,),
        in_specs=[row(D_MODEL), row(H_A * D_VA), row(W_B), row(D_PLE), st_spec]
                 + [_const_spec(w.shape) for w in weights],
        out_specs=(row(D_MODEL), st_spec),
        out_shape=(jax.ShapeDtypeStruct((t, D_MODEL), F32),
                   jax.ShapeDtypeStruct((n_seq, CONV_W - 1, 2 * D_FF), F32)),
        scratch_shapes=[pltpu.VMEM((nb, CONV_W - 1, 2 * D_FF), F32),
                        pltpu.VMEM((FC // LANES, nb, 2 * (s + 8), LANES), F32),
                        pltpu.VMEM((tm, D_FF), BF16)],
        compiler_params=pltpu.CompilerParams(dimension_semantics=("arbitrary",),
                                             vmem_limit_bytes=VMEM_LIMIT),
        name="post_ffn",
    )(x2d, oa, ob, p2d, state, *weights)


def _rope_angles(pos):
    half = D_ROPE // 2
    inv = 1.0 / (ROPE_BASE ** (jnp.arange(half, dtype=F32) / half))
    ang = pos.astype(F32)[:, None] * inv[None, :]
    return jnp.cos(ang), jnp.sin(ang)


def _rope_tables(pos):
    cos, sin = _rope_angles(pos)
    n = pos.shape[0]
    ct = jnp.concatenate([jnp.ones((n, D_NOPE), F32), cos, cos, jnp.zeros((n, LANES - D_NOPE - D_ROPE), F32)], 1)
    st = jnp.concatenate([jnp.zeros((n, D_NOPE), F32), sin, sin, jnp.zeros((n, LANES - D_NOPE - D_ROPE), F32)], 1)
    return ct, st


def _rope_tables_t(pos):
    cos, sin = _rope_angles(pos)
    return jnp.concatenate([cos, cos], 1).T, jnp.concatenate([sin, sin], 1).T


def _rot_half_cols(w):
    half = D_ROPE // 2
    return jnp.concatenate([-w[..., half:], w[..., :half]], -1)


def _prep_weights(w_in, w_uq, w_uk, w_uv):
    o1, o2, o3 = D_CQ, D_CQ + D_CKV, D_CQ + D_CKV + D_ROPE
    o4, o5 = o3 + W_B, o3 + 2 * W_B
    w_kr = w_in[:, o2:o3]
    w_dq, w_dk, w_dv = w_in[:, o3:o4], w_in[:, o4:o5], w_in[:, o5:]
    z = lambda c: jnp.zeros((D_MODEL, c), F32)
    w1p = jnp.concatenate([w_in[:, :o2], w_dq, w_dv], 1).astype(BF16)
    wtp = jnp.concatenate([w_dk, w_kr, _rot_half_cols(w_kr)], 1).T.astype(BF16)
    w1s = jnp.concatenate([w_in[:, :o2], w_dq, w_dk, w_dv,
                           z(D_NOPE), w_kr, z(LANES - D_NOPE - D_ROPE),
                           z(D_NOPE), _rot_half_cols(w_kr), z(LANES - D_NOPE - D_ROPE)], 1).astype(BF16)
    wq3 = w_uq.reshape(D_CQ, H_A, D_NOPE + D_ROPE)
    zq = lambda c: jnp.zeros((D_CQ, H_A, c), F32)
    wq_a = jnp.concatenate([wq3, zq(LANES - D_NOPE - D_ROPE)], -1).reshape(D_CQ, H_A * LANES)
    wq_b = jnp.concatenate([zq(D_NOPE), _rot_half_cols(wq3[..., D_NOPE:]), zq(LANES - D_NOPE - D_ROPE)],
                           -1).reshape(D_CQ, H_A * LANES)
    wq = jnp.concatenate([wq_a, wq_b], 1).astype(BF16)
    wk3 = w_uk.reshape(D_CKV, H_A, D_NOPE)
    wabs_lat = jnp.concatenate([jnp.transpose(wk3, (1, 2, 0)),
                                jnp.zeros((H_A, LANES - D_NOPE, D_CKV), F32)], 1)
    pick = jnp.zeros((LANES, LANES), F32).at[D_NOPE + jnp.arange(D_ROPE), jnp.arange(D_ROPE)].set(1.0)
    wabs = jnp.concatenate([wabs_lat, jnp.broadcast_to(pick, (H_A, LANES, LANES))], -1)
    wabs = wabs.reshape(H_A * LANES, D_CKV + LANES).astype(BF16)
    wv3 = w_uv.reshape(D_CKV, H_A, D_VA)
    eye = jnp.eye(H_A, dtype=F32)
    wuvb = jnp.einsum('lhd,hg->hlgd', wv3, eye).reshape(H_A * D_CKV, H_A * D_VA).astype(BF16)
    return w1p, wtp, w1s, wq, w_uk.T.astype(BF16), w_uv.astype(BF16), wabs, wuvb


def _alibi_tables(s):
    j = jnp.arange(s, dtype=jnp.int32)
    j_hi = ((j // 256) * 256).astype(F32)
    j_lo = (j % 256).astype(F32)
    kaug = jnp.zeros((D_HB, s), F32).at[0:N_AUG].set(jnp.stack([j_hi, j_lo] * 3)).astype(BF16)
    c_hi = jnp.asarray(LOG2E, F32).astype(BF16).astype(F32)
    c_mid = (jnp.asarray(LOG2E, F32) - c_hi).astype(BF16).astype(F32)
    c_lo = (jnp.asarray(LOG2E, F32) - c_hi - c_mid).astype(BF16).astype(F32)
    consts = jnp.stack([c_hi, c_hi, c_mid, c_mid, c_lo, c_lo])
    slopes = jnp.asarray([2.0 ** (-8.0 * (h + 1) / H_B) for h in range(H_B)], F32)
    vals = slopes[:, None] * consts[None, :]
    qaug = jnp.zeros((H_B, 2, 8, LANES), F32)
    qaug = qaug.at[:, 0, :, D_HB:D_HB + N_AUG].set(jnp.broadcast_to(vals[:, None, :], (H_B, 8, N_AUG)))
    qaug = qaug.at[:, 1, :, 0:N_AUG].set(jnp.broadcast_to(vals[:, None, :], (H_B, 8, N_AUG)))
    return kaug, qaug, jnp.broadcast_to(slopes[:, None, None], (H_B, 8, LANES))


def kernel(x_prompt, x_sample, cache_ckv, cache_krope, cache_diff_k, cache_diff_v, state_ffn_conv, p_prompt, p_sample, w_in, g_cq, w_uq, g_ckv, w_uk, w_uv, lambda_q1, lambda_k1, lambda_q2, lambda_k2, g_subln, w_o, ln1_g, ln1_b, w_up, conv_w, conv_b, w_down, ln2_g, ln2_b, w_ple_gate, b_ple_gate, w_ple_proj):
    b, s, _ = x_prompt.shape
    bs, n, _ = x_sample.shape
    past_len = cache_ckv.shape[2]
    w1p, wtp, w1s, wq, wukt, wv, wabs, wuvb = _prep_weights(w_in[0], w_uq[0], w_uk[0], w_uv[0])
    post_w = (w_o[0][:H_A * D_VA].astype(BF16), w_o[0][H_A * D_VA:].astype(BF16), ln1_g, ln1_b,
              w_up[0].astype(BF16), conv_w[0], conv_b, w_down[0].astype(BF16), ln2_g, ln2_b,
              w_ple_gate[0].astype(BF16), b_ple_gate, w_ple_proj[0].astype(BF16))
    kaug, qaug, slopes = _alibi_tables(s)
    lams = (lambda_q1, lambda_k1, lambda_q2, lambda_k2)

    tm = 512
    pos = jnp.arange(s)
    tabs = _rope_tables(pos) + _rope_tables_t(pos)
    x2p = x_prompt.reshape(b * s, D_MODEL)
    (ckv_p, krt_p, dkt_p, dv4_p, qa, knt, krtb, va, dq, dktb, dvb) = _pre_t_call(
        x2p, b, s, tabs, (w1p, wtp, wq, wukt, wv, g_cq, g_ckv), tm)
    r3 = lambda a: a.reshape(b, s, a.shape[-1])
    oa = _attn_a_call(r3(qa), knt, krtb, r3(va))
    ob = _attn_b_call(slopes, qaug, kaug, *lams, g_subln, r3(dq), dktb, r3(dvb))
    zero_state = jnp.zeros((b, CONV_W - 1, 2 * D_FF), F32)
    y_p, conv_p = _post_call(x2p, oa.reshape(b * s, -1), ob.reshape(b * s, -1),
                             p_prompt[0].reshape(b * s, D_PLE), zero_state, post_w, tm, 1, s // tm)

    nbs = 8
    tms = nbs * n
    cts, sts = _rope_tables(past_len + jnp.arange(n))
    cts, sts = jnp.tile(cts, (nbs, 1)), jnp.tile(sts, (nbs, 1))
    x2s = x_sample.reshape(bs * n, D_MODEL)
    (ckv_s, kr_s, ndk_s, ndv_s, qa_s, dq_s, dk_s, dv_s) = _pre_s_call(x2s, cts, sts, w1s, wq, g_cq, g_ckv, tms)
    q3 = lambda a: a.reshape(bs, n, a.shape[-1])
    ckrt = jnp.transpose(cache_krope[0], (0, 2, 1))
    cdkt = jnp.transpose(cache_diff_k[0], (0, 2, 3, 4, 1)).reshape(bs, W_B, past_len)
    cdv4 = cache_diff_v[0].reshape(bs, past_len * H_B, 2 * D_HB)
    oa_s, ob_s = _attn_s_call(
        q3(qa_s), q3(dq_s), cache_ckv[0], ckrt, cdkt, cdv4,
        q3(ckv_s), q3(kr_s), q3(dk_s), q3(dv_s), wabs, wuvb, *lams, g_subln)
    y_s, conv_s = _post_call(x2s, oa_s.reshape(bs * n, -1), ob_s.reshape(bs * n, -1),
                             p_sample[0].reshape(bs * n, D_PLE), state_ffn_conv[0], post_w, tms, nbs, 1)

    new_kr_p = jnp.transpose(krt_p, (0, 2, 1))[None]
    new_dk_p = jnp.transpose(dkt_p.reshape(b, H_B, 2, D_HB, s), (0, 4, 1, 2, 3))[None]
    return (y_p.reshape(b, s, D_MODEL), y_s.reshape(bs, n, D_MODEL),
            ckv_p.reshape(1, b, s, D_CKV), new_kr_p, new_dk_p,
            dv4_p.reshape(1, b, s, H_B, 2 * D_HB),
            conv_p[None],
            ckv_s.reshape(1, bs, n, D_CKV), kr_s.reshape(1, bs, n, D_ROPE),
            ndk_s.reshape(1, bs, n, H_B, 2, D_HB), ndv_s.reshape(1, bs, n, H_B, 2 * D_HB),
            conv_s[None])
```

```python
import functools
import math

import numpy as np
import jax
import jax.numpy as jnp
from jax import lax
from jax.experimental import pallas as pl
from jax.experimental.pallas import tpu as pltpu

F32 = jnp.float32
BF16 = jnp.bfloat16

D_MODEL = 1024
DEPTH = 1
CHUNK = 64
H_A = 8
D_NOPE = 64
D_ROPE = 32
D_VA = 64
D_CQ = 256
D_CKV = 256
ROPE_BASE = 10000.0
H_B = 4
D_HB = 64
W_B = H_B * 2 * D_HB
D_FF = 2816
CONV_W = 3
D_PLE = 256
ALPHA = (2 * DEPTH) ** 0.25
EPS = 1e-5
LAM_INIT = 0.8 - 0.6 * math.exp(-0.3 * 0)
LOG2E = 1.4426950408889634
SCALE_A = (D_NOPE + D_ROPE) ** -0.5 * LOG2E
SCALE_B = D_HB ** -0.5 * LOG2E

LANES = 128
VMEM_LIMIT = 56 * 1024 * 1024
TQ = 512
HQ = TQ // 2
FC = 256
N_FC = D_FF // FC
N_AUG = 6


def _dot(a, b):
    return jnp.dot(a, b, preferred_element_type=F32)


def _dot_nt(a, b):
    return lax.dot_general(a, b, (((1,), (1,)), ((), ())), preferred_element_type=F32)


def _const_spec(shape):
    nd = len(shape)
    return pl.BlockSpec(shape, lambda *_: (0,) * nd, pipeline_mode=pl.Buffered(1))


def _rms(v, g):
    return v * lax.rsqrt(jnp.mean(v * v, -1, keepdims=True) + EPS) * g


def _gelu(v):
    return 0.5 * v * (1.0 + lax.erf(v * (2.0 ** -0.5)))


def _ln(v, g, b):
    mu = jnp.mean(v, -1, keepdims=True)
    d = v - mu
    var = jnp.mean(d * d, -1, keepdims=True)
    return d * lax.rsqrt(var + EPS) * g + b


def _rope_q(qq, ct, st, qa_ref):
    for h in range(H_A):
        sl = slice(h * LANES, (h + 1) * LANES)
        qh = qq[:, sl] * ct + qq[:, H_A * LANES + h * LANES:H_A * LANES + (h + 1) * LANES] * st
        qa_ref[:, sl] = (qh * SCALE_A).astype(BF16)


def _pre_t_kernel(x_ref, ctt_ref, stt_ref, w1_ref, wt_ref, wqt_ref, wk_ref, wuvt_ref, gcq_ref, gckv_ref,
                  ckv_ref, krt_ref, dkt_ref, dv4_ref,
                  qat_ref, ka_ref, vat_ref, dqt_ref, dk_ref, dvt_ref):
    tm = x_ref.shape[0]
    xb = x_ref[...].astype(BF16)
    proj = _dot(xb, w1_ref[...])
    projt = _dot_nt(wt_ref[...], xb)
    cqn = _rms(proj[:, 0:D_CQ], gcq_ref[...])
    ckvn = _rms(proj[:, D_CQ:D_CQ + D_CKV], gckv_ref[...])
    ckv_ref[...] = ckvn
    dv = proj[:, 512:1024]
    for h in range(H_B):
        dv4_ref[pl.ds(h, tm, stride=H_B), :] = dv[:, h * LANES:(h + 1) * LANES]
    dvt_ref[0] = dv.T.astype(BF16)
    dqt_ref[0] = (projt[0:W_B] * SCALE_B).astype(BF16)
    dkt = projt[W_B:2 * W_B]
    dkt_ref[0] = dkt
    dk_ref[...] = dkt.T.astype(BF16)
    ctt = ctt_ref[...]
    stt = stt_ref[...]
    r0, r1 = D_NOPE, D_NOPE + D_ROPE
    krt = projt[2 * W_B:2 * W_B + D_ROPE] * ctt[r0:r1] + projt[2 * W_B + D_ROPE:2 * W_B + 2 * D_ROPE] * stt[r0:r1]
    krt_ref[0] = krt
    krp = jnp.concatenate([jnp.zeros((r0, tm), F32), krt, jnp.zeros((LANES - r1, tm), F32)], axis=0).T
    ckvb = ckvn.astype(BF16)
    kn = _dot(ckvb, wk_ref[...])
    vat_ref[0] = _dot_nt(wuvt_ref[...], ckvb).astype(BF16)
    qqt = _dot_nt(wqt_ref[...], cqn.astype(BF16))
    for h in range(H_A):
        sl = slice(h * LANES, (h + 1) * LANES)
        ka_ref[:, sl] = (kn[:, sl] + krp).astype(BF16)
        qh = qqt[sl] * ctt + qqt[H_A * LANES + h * LANES:H_A * LANES + (h + 1) * LANES] * stt
        qat_ref[0, sl, :] = (qh * SCALE_A).astype(BF16)


def _pre_t_call(x2d, b, s, tabs, weights, tm):
    t = x2d.shape[0]
    nt = s // tm
    row = lambda c: pl.BlockSpec((tm, c), lambda i, j: (i * nt + j, 0))
    tabt = pl.BlockSpec((LANES, tm), lambda i, j: (0, j))
    tr = lambda r: pl.BlockSpec((1, r, tm), lambda i, j: (i, 0, j))
    out_shapes = (
        jax.ShapeDtypeStruct((t, D_CKV), F32), jax.ShapeDtypeStruct((b, D_ROPE, s), F32),
        jax.ShapeDtypeStruct((b, W_B, s), F32), jax.ShapeDtypeStruct((t * H_B, LANES), F32),
        jax.ShapeDtypeStruct((b, H_A * LANES, s), BF16), jax.ShapeDtypeStruct((t, H_A * LANES), BF16),
        jax.ShapeDtypeStruct((b, H_A * D_VA, s), BF16), jax.ShapeDtypeStruct((b, W_B, s), BF16),
        jax.ShapeDtypeStruct((t, W_B), BF16), jax.ShapeDtypeStruct((b, W_B, s), BF16))
    dv4_spec = pl.BlockSpec((tm * H_B, LANES), lambda i, j: (i * nt + j, 0))
    return pl.pallas_call(
        _pre_t_kernel,
        grid=(b, nt),
        in_specs=[row(D_MODEL), tabt, tabt] + [_const_spec(w.shape) for w in weights],
        out_specs=(row(D_CKV), tr(D_ROPE), tr(W_B), dv4_spec, tr(H_A * LANES), row(H_A * LANES),
                   tr(H_A * D_VA), tr(W_B), row(W_B), tr(W_B)),
        out_shape=out_shapes,
        compiler_params=pltpu.CompilerParams(dimension_semantics=("arbitrary", "arbitrary"),
                                             vmem_limit_bytes=VMEM_LIMIT),
        name="pre_proj_prompt",
    )(x2d, *tabs, *weights)


def _pre_s_kernel(x_ref, ct_ref, st_ref, w1_ref, wq_ref, gcq_ref, gckv_ref,
                  ckv_ref, krope_ref, ndk_ref, ndv_ref, qa_ref, dq_ref, dk_ref, dv_ref):
    proj = _dot(x_ref[...].astype(BF16), w1_ref[...])
    cqn = _rms(proj[:, 0:D_CQ], gcq_ref[...])
    ckv_ref[...] = _rms(proj[:, D_CQ:D_CQ + D_CKV], gckv_ref[...])
    dq = proj[:, 512:1024]
    dk = proj[:, 1024:1536]
    dv = proj[:, 1536:2048]
    ndk_ref[...] = dk
    ndv_ref[...] = dv
    dq_ref[...] = (dq * SCALE_B).astype(BF16)
    dk_ref[...] = dk.astype(BF16)
    dv_ref[...] = dv.astype(BF16)
    ct = ct_ref[...]
    st = st_ref[...]
    krp = proj[:, 2048:2048 + LANES] * ct + proj[:, 2048 + LANES:2048 + 2 * LANES] * st
    krope_ref[...] = pltpu.roll(krp, 64, 1)[:, 0:D_ROPE]
    _rope_q(_dot(cqn.astype(BF16), wq_ref[...]), ct, st, qa_ref)


def _pre_s_call(x2d, ct, st, w1, wq, gcq, gckv, tm):
    t = x2d.shape[0]
    row = lambda c: pl.BlockSpec((tm, c), lambda i: (i, 0))
    tab = pl.BlockSpec((tm, LANES), lambda i: (0, 0))
    out_shapes = (
        jax.ShapeDtypeStruct((t, D_CKV), F32), jax.ShapeDtypeStruct((t, D_ROPE), F32),
        jax.ShapeDtypeStruct((t, W_B), F32), jax.ShapeDtypeStruct((t, W_B), F32),
        jax.ShapeDtypeStruct((t, H_A * LANES), BF16), jax.ShapeDtypeStruct((t, W_B), BF16),
        jax.ShapeDtypeStruct((t, W_B), BF16), jax.ShapeDtypeStruct((t, W_B), BF16))
    return pl.pallas_call(
        _pre_s_kernel,
        grid=(t // tm,),
        in_specs=[row(D_MODEL), tab, tab, _const_spec(w1.shape), _const_spec(wq.shape),
                  _const_spec(gcq.shape), _const_spec(gckv.shape)],
        out_specs=(row(D_CKV), row(D_ROPE), row(W_B), row(W_B), row(H_A * LANES), row(W_B), row(W_B), row(W_B)),
        out_shape=out_shapes,
        compiler_params=pltpu.CompilerParams(dimension_semantics=("arbitrary",),
                                             vmem_limit_bytes=VMEM_LIMIT),
        name="pre_proj_sample",
    )(x2d, ct, st, w1, wq, gcq, gckv)


def _chunk_mask_t():
    r = lax.broadcasted_iota(jnp.int32, (HQ, HQ), 0) // CHUNK
    c = lax.broadcasted_iota(jnp.int32, (HQ, HQ), 1) // CHUNK
    return r <= c


def _rowmax(x):
    return jnp.max(x, -1, keepdims=True)


def _rowsum(x):
    return jnp.sum(x, -1, keepdims=True)


def _colmax(x):
    return jnp.max(x, 0, keepdims=True)


def _colsum(x):
    return jnp.sum(x, 0, keepdims=True)


def _scores_t(k_ref, mi, qt, r0, mask, corr):
    so = _dot(k_ref[mi, 0:r0, :], qt) if r0 > 0 else None
    s0 = _dot(k_ref[mi, r0:r0 + HQ, :], qt)
    s1 = _dot(k_ref[mi, r0 + HQ:r0 + TQ, :], qt[:, HQ:TQ])
    s0l = s0[:, 0:HQ]
    if corr is not None:
        s0l = s0l - corr
        s1 = s1 - corr
    s0 = jnp.concatenate([jnp.where(mask, s0l, -jnp.inf), s0[:, HQ:TQ]], axis=1)
    return so, s0, jnp.where(mask, s1, -jnp.inf)


def _probs_t(sc):
    so, s0, s1 = sc
    m = _colmax(s0)
    if so is not None:
        m = jnp.maximum(m, _colmax(so))
    mr = jnp.maximum(m[:, HQ:TQ], _colmax(s1))
    m = jnp.concatenate([m[:, 0:HQ], mr], axis=1)
    p0 = jnp.exp2(s0 - m)
    p1 = jnp.exp2(s1 - mr)
    l = _colsum(p0) + jnp.concatenate([jnp.zeros((1, HQ), F32), _colsum(p1)], axis=1)
    p_right = jnp.concatenate([p0[:, HQ:TQ], p1], axis=0).astype(BF16)
    p_left = p0[:, 0:HQ].astype(BF16)
    if so is None:
        return None, p_left, p_right, l
    po = jnp.exp2(so - m)
    return po.astype(BF16), p_left, p_right, l + _colsum(po)


def _weighted_values_t(pr, vt_ref, r0):
    po, p_left, p_right, l = pr
    o = jnp.concatenate([_dot(vt_ref[0, :, r0:r0 + HQ], p_left), _dot(vt_ref[0, :, r0:r0 + TQ], p_right)], axis=1)
    if po is not None:
        o = o + _dot(vt_ref[0, :, 0:r0], po)
    return o, l


def _attend_pipelined(n_tiles, k_ref, get_qt, vt_ref, mask, corr, finish, ahead):
    chains = [(qi, mi) for qi in range(n_tiles) for mi in range(2)]
    sc, res = {}, {}

    def scores(c):
        qi, mi = chains[c]
        sc[c] = _scores_t(k_ref, mi, get_qt(qi, mi), qi * TQ, mask, corr)

    for c in range(min(ahead, len(chains))):
        scores(c)
    for c, (qi, mi) in enumerate(chains):
        pr = _probs_t(sc.pop(c))
        if c + ahead < len(chains):
            scores(c + ahead)
        res[c] = _weighted_values_t(pr, vt_ref, qi * TQ)
        if mi == 1:
            finish(qi, res.pop(c - 1), res.pop(c))


def _attn_a_kernel(k1_ref, k2_ref, qt_ref, vt_ref, o_ref, k_ref):
    s = k1_ref.shape[1]
    k_ref[0] = k1_ref[0]
    k_ref[1] = k2_ref[0]
    row = lax.broadcasted_iota(jnp.int32, (LANES, TQ), 0)

    def get_qt(qi, mi):
        return qt_ref[0, mi * LANES:(mi + 1) * LANES, qi * TQ:(qi + 1) * TQ]

    def finish(qi, r1, r2):
        ot = jnp.where(row < D_VA, r1[0] / r1[1], r2[0] / r2[1])
        o_ref[0, qi * TQ:(qi + 1) * TQ, :] = ot.T.astype(BF16)

    _attend_pipelined(s // TQ, k_ref, get_qt, vt_ref, _chunk_mask_t(), None, finish, ahead=2)


def _lambda_value(lq1_ref, lk1_ref, lq2_ref, lk2_ref):
    a = jnp.sum(lq1_ref[...] * lk1_ref[...], -1, keepdims=True)
    b = jnp.sum(lq2_ref[...] * lk2_ref[...], -1, keepdims=True)
    return jnp.exp(a) - jnp.exp(b) + LAM_INIT


def _attn_b_kernel(slope_ref, qaug_ref, kaug_ref, lq1_ref, lk1_ref, lq2_ref, lk2_ref, g_ref,
                   dk_ref, dqt_ref, vt_ref, o_ref, k_ref, qt_ref):
    s = dk_ref.shape[1]
    lane = lax.broadcasted_iota(jnp.int32, (s, LANES), 1)
    dk = dk_ref[0]
    k_ref[0] = jnp.where(lane < D_HB, dk, kaug_ref[0])
    k_ref[1] = jnp.where(lane >= D_HB, dk, kaug_ref[1])
    qt_ref[0, 0:D_HB, :] = dqt_ref[0, 0:D_HB, :]
    qt_ref[0, D_HB:LANES, :] = qaug_ref[0]
    qt_ref[1, 0:D_HB, :] = qaug_ref[0]
    qt_ref[1, D_HB:LANES, :] = dqt_ref[0, D_HB:LANES, :]
    slope2 = slope_ref[0, 0:1, 0:1] * (2.0 * LOG2E)
    lam = _lambda_value(lq1_ref, lk1_ref, lq2_ref, lk2_ref)
    ri = lax.broadcasted_iota(jnp.int32, (HQ, HQ), 0)
    ci = lax.broadcasted_iota(jnp.int32, (HQ, HQ), 1)
    corr = slope2 * jnp.maximum(ri - ci, 0).astype(F32)

    def get_qt(qi, mi):
        return qt_ref[mi, :, qi * TQ:(qi + 1) * TQ]

    def finish(qi, r1, r2):
        o = (r1[0] / r1[1] - lam * (r2[0] / r2[1])).T
        o_ref[0, qi * TQ:(qi + 1) * TQ, :] = (_rms(o, g_ref[...]) * (1.0 - LAM_INIT)).astype(BF16)

    _attend_pipelined(s // TQ, k_ref, get_qt, vt_ref, _chunk_mask_t(), corr, finish, ahead=3)


def _attn_a_call(ka, qat, vat):
    b, s, _ = ka.shape
    blk = lambda f: pl.BlockSpec((1, s, LANES), f)
    return pl.pallas_call(
        _attn_a_kernel,
        grid=(b, H_A // 2),
        in_specs=[blk(lambda i, j: (i, 0, 2 * j)), blk(lambda i, j: (i, 0, 2 * j + 1)),
                  pl.BlockSpec((1, 2 * LANES, s), lambda i, j: (i, j, 0)),
                  pl.BlockSpec((1, LANES, s), lambda i, j: (i, j, 0))],
        out_specs=blk(lambda i, j: (i, 0, j)),
        out_shape=jax.ShapeDtypeStruct((b, s, H_A * D_VA), BF16),
        scratch_shapes=[pltpu.VMEM((2, s, LANES), BF16)],
        compiler_params=pltpu.CompilerParams(dimension_semantics=("arbitrary", "arbitrary"),
                                             vmem_limit_bytes=VMEM_LIMIT),
        name="attn_mla",
    )(ka, ka, qat, vat)


def _attn_b_call(slopes, qaug, kaug, lq1, lk1, lq2, lk2, g, dk, dqt, dvt):
    b, s, _ = dk.shape
    blk = pl.BlockSpec((1, s, LANES), lambda i, j: (i, 0, j))
    blkt = pl.BlockSpec((1, LANES, s), lambda i, j: (i, j, 0))
    vec = lambda a: pl.BlockSpec(a.shape, lambda i, j: (0,) * a.ndim)
    return pl.pallas_call(
        _attn_b_kernel,
        grid=(b, H_B),
        in_specs=[pl.BlockSpec((1, 8, LANES), lambda i, j: (j, 0, 0)),
                  pl.BlockSpec((1, D_HB, s), lambda i, j: (j, 0, 0)),
                  vec(kaug), vec(lq1), vec(lk1), vec(lq2), vec(lk2), vec(g),
                  blk, blkt, blkt],
        out_specs=blk,
        out_shape=jax.ShapeDtypeStruct((b, s, W_B), BF16),
        scratch_shapes=[pltpu.VMEM((2, s, LANES), BF16), pltpu.VMEM((2, LANES, s), BF16)],
        compiler_params=pltpu.CompilerParams(dimension_semantics=("arbitrary", "arbitrary"),
                                             vmem_limit_bytes=VMEM_LIMIT),
        name="attn_diff",
    )(slopes, qaug, kaug, lq1, lk1, lq2, lk2, g, dk, dqt, dvt)


def _stack_rows(x, reps, width):
    n, c = x.shape
    t = jnp.concatenate([x] * reps, axis=0)
    rb = lax.broadcasted_iota(jnp.int32, (reps * n, c), 0) // n
    cb = lax.broadcasted_iota(jnp.int32, (reps * n, c), 1) // width
    return jnp.where(rb == cb, t, jnp.zeros_like(t))


def _softmax2(sc, sn):
    m = jnp.maximum(_rowmax(sc), _rowmax(sn))
    pc = jnp.exp2(sc - m)
    pn = jnp.exp2(sn - m)
    l = _rowsum(pc) + _rowsum(pn)
    return pc.astype(BF16), pn.astype(BF16), l


def _attn_s_kernel(past_len, qa_ref, dq_ref, cckv_ref, ckrt_ref, cdkt_ref, cdv4_ref,
                   nckv_ref, nkr_ref, ndk_ref, ndv_ref, wabs_ref, wuvb_ref,
                   lq1_ref, lk1_ref, lq2_ref, lk2_ref, g_ref, oa_ref, ob_ref):
    n = qa_ref.shape[1]
    qblk = _stack_rows(qa_ref[0], H_A, LANES)
    g = _dot(qblk, wabs_ref[...])
    qabs = g[:, 0:D_CKV].astype(BF16)
    qrp = g[:, D_CKV:D_CKV + D_ROPE].astype(BF16)
    cc = cckv_ref[0].astype(BF16)
    cn = nckv_ref[0].astype(BF16)
    sc = _dot_nt(qabs, cc) + _dot(qrp, ckrt_ref[0].astype(BF16))
    sn = _dot_nt(qabs, cn) + _dot_nt(qrp, nkr_ref[0].astype(BF16))
    pc, pn, l = _softmax2(sc, sn)
    ctx = ((_dot(pc, cc) + _dot(pn, cn)) / l).astype(BF16)
    oa = _dot(ctx[0:n], wuvb_ref[0:D_CKV, :])
    for h in range(1, H_A):
        oa = oa + _dot(ctx[h * n:(h + 1) * n], wuvb_ref[h * D_CKV:(h + 1) * D_CKV, :])
    oa_ref[0] = oa.astype(BF16)
    rows = 2 * H_B * n
    qb = _stack_rows(dq_ref[0], 2 * H_B, D_HB)
    sc = _dot(qb, cdkt_ref[0].astype(BF16))
    sn = _dot_nt(qb, ndk_ref[0])
    r = lax.broadcasted_iota(jnp.int32, (rows, 1), 0)
    hh = r // (2 * n)
    slope = jnp.where(hh == 0, 2.0 ** -2, jnp.where(hh == 1, 2.0 ** -4,
                      jnp.where(hh == 2, 2.0 ** -6, 2.0 ** -8))).astype(F32) * LOG2E
    qpos = (r % n).astype(F32)
    kc = lax.broadcasted_iota(jnp.int32, (1, past_len), 1).astype(F32)
    kn = lax.broadcasted_iota(jnp.int32, (1, n), 1).astype(F32)
    sc = sc - slope * jnp.abs((qpos + float(past_len)) - kc)
    sn = sn - slope * jnp.abs(qpos - kn)
    pc, pn, l = _softmax2(sc, sn)
    lam = _lambda_value(lq1_ref, lk1_ref, lq2_ref, lk2_ref)
    inv_l = 1.0 / l
    for h in range(H_B):
        sl = slice(h * LANES, (h + 1) * LANES)
        rs = slice(2 * h * n, (2 * h + 2) * n)
        vh = cdv4_ref[0, pl.ds(h, past_len, stride=H_B), :].astype(BF16)
        of = (_dot(pc[rs], vh) + _dot(pn[rs], ndv_ref[0, :, sl])) * inv_l[rs]
        o = of[0:n] - lam * of[n:2 * n]
        ob_ref[0, :, sl] = (_rms(o, g_ref[...]) * (1.0 - LAM_INIT)).astype(BF16)


def _attn_s_call(qa, dq, cckv, ckrt, cdkt, cdv4, nckv, nkr, ndk, ndv, wabs, wuvb, lq1, lk1, lq2, lk2, g):
    b, n, _ = qa.shape
    past_len = cckv.shape[1]
    per_b = lambda a: pl.BlockSpec((1,) + a.shape[1:], lambda i: (i, 0, 0))
    vec = lambda a: pl.BlockSpec(a.shape, lambda i: (0,) * a.ndim)
    args = (qa, dq, cckv, ckrt, cdkt, cdv4, nckv, nkr, ndk, ndv)
    consts = (wabs, wuvb)
    vecs = (lq1, lk1, lq2, lk2, g)
    return pl.pallas_call(
        functools.partial(_attn_s_kernel, past_len),
        grid=(b,),
        in_specs=[per_b(a) for a in args] + [_const_spec(a.shape) for a in consts] + [vec(a) for a in vecs],
        out_specs=(pl.BlockSpec((1, n, H_A * D_VA), lambda i: (i, 0, 0)),
                   pl.BlockSpec((1, n, W_B), lambda i: (i, 0, 0))),
        out_shape=(jax.ShapeDtypeStruct((b, n, H_A * D_VA), BF16), jax.ShapeDtypeStruct((b, n, W_B), BF16)),
        compiler_params=pltpu.CompilerParams(dimension_semantics=("arbitrary",),
                                             vmem_limit_bytes=VMEM_LIMIT),
        name="attn_sample",
    )(*args, *consts, *vecs)


def _post_kernel(nb, tiles_per_seq, x_ref, oa_ref, ob_ref, p_ref, state_ref,
                 woa_ref, wob_ref, ln1g_ref, ln1b_ref, wup_ref, cw_ref, cb_ref, wdn_ref,
                 ln2g_ref, ln2b_ref, wg_ref, bg_ref, wp_ref,
                 y_ref, nconv_ref, carry_ref, ush_ref, act_ref):
    tm = x_ref.shape[0]
    s = tm // nb
    i = pl.program_id(0)

    @pl.when(i % tiles_per_seq == 0)
    def _():
        carry_ref[...] = state_ref[...]

    a = _dot(oa_ref[...], woa_ref[...]) + _dot(ob_ref[...], wob_ref[...])
    h = _ln(ALPHA * x_ref[...] + a, ln1g_ref[...], ln1b_ref[...])
    hb = h.astype(BF16)

    def conv(c0):
        u = _dot(hb, wup_ref[:, c0:c0 + FC])
        u3 = u.reshape(nb, s, FC)
        tail = u3[:, s - 2:s, :]
        z = cb_ref[:, c0:c0 + FC] + u3 * cw_ref[2:3, c0:c0 + FC]
        zs = []
        for k in range(FC // LANES):
            cl = slice(c0 + k * LANES, c0 + (k + 1) * LANES)
            sl = slice(k * LANES, (k + 1) * LANES)
            for b in range(nb):
                ush_ref[k, b, pl.ds(2 * 6, 2, stride=2), :] = carry_ref[b, :, cl]
                ush_ref[k, b, pl.ds(2 * 8, s, stride=2), :] = u3[b, :, sl]
            um2 = jnp.stack([ush_ref[k, b, pl.ds(2 * 6, s, stride=2), :] for b in range(nb)])
            um1 = jnp.stack([ush_ref[k, b, pl.ds(2 * 7, s, stride=2), :] for b in range(nb)])
            zs.append(z[:, :, sl] + um2 * cw_ref[0:1, cl] + um1 * cw_ref[1:2, cl])
        carry_ref[:, :, c0:c0 + FC] = tail
        nconv_ref[:, :, c0:c0 + FC] = tail
        return jnp.concatenate(zs, axis=-1).reshape(tm, FC)

    for c in range(N_FC):
        zg = conv(c * FC)
        zv = conv(D_FF + c * FC)
        act_ref[:, c * FC:(c + 1) * FC] = (_gelu(zg) * zv).astype(BF16)

    f = _dot(act_ref[...], wdn_ref[...])
    h2 = _ln(ALPHA * h + f, ln2g_ref[...], ln2b_ref[...])
    gate = jax.nn.sigmoid(_dot(h2.astype(BF16), wg_ref[...]) + bg_ref[...])
    y_ref[...] = h2 + gate * _dot(p_ref[...].astype(BF16), wp_ref[...])


def _post_call(x2d, oa, ob, p2d, state, weights, tm, nb, tiles_per_seq):
    t = x2d.shape[0]
    n_seq = state.shape[0]
    s = tm // nb
    row = lambda c: pl.BlockSpec((tm, c), lambda i: (i, 0))
    st_spec = pl.BlockSpec((nb, CONV_W - 1, 2 * D_FF), lambda i: (i // tiles_per_seq, 0, 0))
    return pl.pallas_call(
        functools.partial(_post_kernel, nb, tiles_per_seq),
        grid=(t // tm,),
        in_specs=[row(D_MODEL), row(H_A * D_VA), row(W_B), row(D_PLE), st_spec]
                 + [_const_spec(w.shape) for w in weights],
        out_specs=(row(D_MODEL), st_spec),
        out_shape=(jax.ShapeDtypeStruct((t, D_MODEL), F32),
                   jax.ShapeDtypeStruct((n_seq, CONV_W - 1, 2 * D_FF), F32)),
        scratch_shapes=[pltpu.VMEM((nb, CONV_W - 1, 2 * D_FF), F32),
                        pltpu.VMEM((FC // LANES, nb, 2 * (s + 8), LANES), F32),
                        pltpu.VMEM((tm, D_FF), BF16)],
        compiler_params=pltpu.CompilerParams(dimension_semantics=("arbitrary",),
                                             vmem_limit_bytes=VMEM_LIMIT),
        name="post_ffn",
    )(x2d, oa, ob, p2d, state, *weights)


def _rope_angles(pos):
    half = D_ROPE // 2
    inv = 1.0 / (ROPE_BASE ** (jnp.arange(half, dtype=F32) / half))
    ang = pos.astype(F32)[:, None] * inv[None, :]
    return jnp.cos(ang), jnp.sin(ang)


def _rope_tables(pos):
    cos, sin = _rope_angles(pos)
    n = pos.shape[0]
    ct = jnp.concatenate([jnp.ones((n, D_NOPE), F32), cos, cos, jnp.zeros((n, LANES - D_NOPE - D_ROPE), F32)], 1)
    st = jnp.concatenate([jnp.zeros((n, D_NOPE), F32), sin, sin, jnp.zeros((n, LANES - D_NOPE - D_ROPE), F32)], 1)
    return ct, st


def _rope_tables_t(pos):
    cos, sin = _rope_angles(pos)
    n = pos.shape[0]
    ctt = jnp.concatenate([jnp.ones((D_NOPE, n), F32), cos.T, cos.T, jnp.zeros((LANES - D_NOPE - D_ROPE, n), F32)], 0)
    stt = jnp.concatenate([jnp.zeros((D_NOPE, n), F32), sin.T, sin.T, jnp.zeros((LANES - D_NOPE - D_ROPE, n), F32)], 0)
    return ctt, stt


def _rot_half_cols(w):
    half = D_ROPE // 2
    return jnp.concatenate([-w[..., half:], w[..., :half]], -1)


def _prep_weights(w_in, w_uq, w_uk, w_uv):
    o1, o2, o3 = D_CQ, D_CQ + D_CKV, D_CQ + D_CKV + D_ROPE
    o4, o5 = o3 + W_B, o3 + 2 * W_B
    w_kr = w_in[:, o2:o3]
    w_dq, w_dk, w_dv = w_in[:, o3:o4], w_in[:, o4:o5], w_in[:, o5:]
    z = lambda c: jnp.zeros((D_MODEL, c), F32)
    w1p = jnp.concatenate([w_in[:, :o2], w_dv], 1).astype(BF16)
    wtp = jnp.concatenate([w_dq, w_dk, w_kr, _rot_half_cols(w_kr)], 1).T.astype(BF16)
    w1s = jnp.concatenate([w_in[:, :o2], w_dq, w_dk, w_dv,
                           z(D_NOPE), w_kr, z(LANES - D_NOPE - D_ROPE),
                           z(D_NOPE), _rot_half_cols(w_kr), z(LANES - D_NOPE - D_ROPE)], 1).astype(BF16)
    wq3 = w_uq.reshape(D_CQ, H_A, D_NOPE + D_ROPE)
    zq = lambda c: jnp.zeros((D_CQ, H_A, c), F32)
    wq_a = jnp.concatenate([wq3, zq(LANES - D_NOPE - D_ROPE)], -1).reshape(D_CQ, H_A * LANES)
    wq_b = jnp.concatenate([zq(D_NOPE), _rot_half_cols(wq3[..., D_NOPE:]), zq(LANES - D_NOPE - D_ROPE)],
                           -1).reshape(D_CQ, H_A * LANES)
    wq = jnp.concatenate([wq_a, wq_b], 1).astype(BF16)
    wk3 = w_uk.reshape(D_CKV, H_A, D_NOPE)
    wk = jnp.concatenate([wk3, jnp.zeros((D_CKV, H_A, LANES - D_NOPE), F32)], -1).reshape(D_CKV, H_A * LANES)
    wabs_lat = jnp.concatenate([jnp.transpose(wk3, (1, 2, 0)),
                                jnp.zeros((H_A, LANES - D_NOPE, D_CKV), F32)], 1)
    pick = jnp.zeros((LANES, LANES), F32).at[D_NOPE + jnp.arange(D_ROPE), jnp.arange(D_ROPE)].set(1.0)
    wabs = jnp.concatenate([wabs_lat, jnp.broadcast_to(pick, (H_A, LANES, LANES))], -1)
    wabs = wabs.reshape(H_A * LANES, D_CKV + LANES).astype(BF16)
    wv3 = w_uv.reshape(D_CKV, H_A, D_VA)
    eye = jnp.eye(H_A, dtype=F32)
    wuvb = jnp.einsum('lhd,hg->hlgd', wv3, eye).reshape(H_A * D_CKV, H_A * D_VA).astype(BF16)
    return w1p, wtp, w1s, wq, wk.astype(BF16), w_uv.T.astype(BF16), wabs, wuvb


def _alibi_tables(s):
    j = jnp.arange(s, dtype=jnp.int32)
    j_hi = ((j // 256) * 256).astype(F32)
    j_lo = (j % 256).astype(F32)
    cols = jnp.stack([j_hi, j_lo] * 3, axis=1)
    kaug = jnp.zeros((2, s, LANES), F32)
    kaug = kaug.at[0, :, D_HB:D_HB + N_AUG].set(cols).at[1, :, 0:N_AUG].set(cols).astype(BF16)
    c_hi = jnp.asarray(LOG2E, F32).astype(BF16).astype(F32)
    c_mid = (jnp.asarray(LOG2E, F32) - c_hi).astype(BF16).astype(F32)
    c_lo = (jnp.asarray(LOG2E, F32) - c_hi - c_mid).astype(BF16).astype(F32)
    consts = jnp.stack([c_hi, c_hi, c_mid, c_mid, c_lo, c_lo])
    slopes = jnp.asarray([2.0 ** (-8.0 * (h + 1) / H_B) for h in range(H_B)], F32)
    vals = slopes[:, None] * consts[None, :]
    qaug = jnp.zeros((H_B, D_HB, s), F32)
    qaug = qaug.at[:, 0:N_AUG, :].set(jnp.broadcast_to(vals[:, :, None], (H_B, N_AUG, s))).astype(BF16)
    return kaug, qaug, jnp.broadcast_to(slopes[:, None, None], (H_B, 8, LANES))


def kernel(x_prompt, x_sample, cache_ckv, cache_krope, cache_diff_k, cache_diff_v, state_ffn_conv, p_prompt, p_sample, w_in, g_cq, w_uq, g_ckv, w_uk, w_uv, lambda_q1, lambda_k1, lambda_q2, lambda_k2, g_subln, w_o, ln1_g, ln1_b, w_up, conv_w, conv_b, w_down, ln2_g, ln2_b, w_ple_gate, b_ple_gate, w_ple_proj):
    b, s, _ = x_prompt.shape
    bs, n, _ = x_sample.shape
    past_len = cache_ckv.shape[2]
    w1p, wtp, w1s, wq, wk, wuvt, wabs, wuvb = _prep_weights(w_in[0], w_uq[0], w_uk[0], w_uv[0])
    post_w = (w_o[0][:H_A * D_VA].astype(BF16), w_o[0][H_A * D_VA:].astype(BF16), ln1_g, ln1_b,
              w_up[0].astype(BF16), conv_w[0], conv_b, w_down[0].astype(BF16), ln2_g, ln2_b,
              w_ple_gate[0].astype(BF16), b_ple_gate, w_ple_proj[0].astype(BF16))
    kaug, qaug, slopes = _alibi_tables(s)
    lams = (lambda_q1, lambda_k1, lambda_q2, lambda_k2)

    tm = 512
    x2p = x_prompt.reshape(b * s, D_MODEL)
    (ckv_p, krt_p, dkt_p, dv4_p, qat, ka, vat, dqt, dk, dvt) = _pre_t_call(
        x2p, b, s, _rope_tables_t(jnp.arange(s)), (w1p, wtp, wq.T, wk, wuvt, g_cq, g_ckv), tm)
    r3 = lambda a: a.reshape(b, s, a.shape[-1])
    oa = _attn_a_call(r3(ka), qat, vat)
    ob = _attn_b_call(slopes, qaug, kaug, *lams, g_subln, r3(dk), dqt, dvt)
    zero_state = jnp.zeros((b, CONV_W - 1, 2 * D_FF), F32)
    y_p, conv_p = _post_call(x2p, oa.reshape(b * s, -1), ob.reshape(b * s, -1),
                             p_prompt[0].reshape(b * s, D_PLE), zero_state, post_w, tm, 1, s // tm)

    nbs = 8
    tms = nbs * n
    cts, sts = _rope_tables(past_len + jnp.arange(n))
    cts, sts = jnp.tile(cts, (nbs, 1)), jnp.tile(sts, (nbs, 1))
    x2s = x_sample.reshape(bs * n, D_MODEL)
    (ckv_s, kr_s, ndk_s, ndv_s, qa_s, dq_s, dk_s, dv_s) = _pre_s_call(x2s, cts, sts, w1s, wq, g_cq, g_ckv, tms)
    q3 = lambda a: a.reshape(bs, n, a.shape[-1])
    ckrt = jnp.transpose(cache_krope[0], (0, 2, 1))
    cdkt = jnp.transpose(cache_diff_k[0], (0, 2, 3, 4, 1)).reshape(bs, W_B, past_len)
    cdv4 = cache_diff_v[0].reshape(bs, past_len * H_B, 2 * D_HB)
    oa_s, ob_s = _attn_s_call(
        q3(qa_s), q3(dq_s), cache_ckv[0], ckrt, cdkt, cdv4,
        q3(ckv_s), q3(kr_s), q3(dk_s), q3(dv_s), wabs, wuvb, *lams, g_subln)
    y_s, conv_s = _post_call(x2s, oa_s.reshape(bs * n, -1), ob_s.reshape(bs * n, -1),
                             p_sample[0].reshape(bs * n, D_PLE), state_ffn_conv[0], post_w, tms, nbs, 1)

    new_kr_p = jnp.transpose(krt_p, (0, 2, 1))[None]
    new_dk_p = jnp.transpose(dkt_p.reshape(b, H_B, 2, D_HB, s), (0, 4, 1, 2, 3))[None]
    return (y_p.reshape(b, s, D_MODEL), y_s.reshape(bs, n, D_MODEL),
            ckv_p.reshape(1, b, s, D_CKV), new_kr_p, new_dk_p,
            dv4_p.reshape(1, b, s, H_B, 2 * D_HB),
            conv_p[None],
            ckv_s.reshape(1, bs, n, D_CKV), kr_s.reshape(1, bs, n, D_ROPE),
            ndk_s.reshape(1, bs, n, H_B, 2, D_HB), ndv_s.reshape(1, bs, n, H_B, 2 * D_HB),
            conv_s[None])
```

```python
import functools
import math

import numpy as np
import jax
import jax.numpy as jnp
from jax import lax
from jax.experimental import pallas as pl
from jax.experimental.pallas import tpu as pltpu

F32 = jnp.float32
BF16 = jnp.bfloat16

D_MODEL = 1024
DEPTH = 1
CHUNK = 64
H_A = 8
D_NOPE = 64
D_ROPE = 32
D_VA = 64
D_CQ = 256
D_CKV = 256
ROPE_BASE = 10000.0
H_B = 4
D_HB = 64
W_B = H_B * 2 * D_HB
D_FF = 2816
CONV_W = 3
D_PLE = 256
ALPHA = (2 * DEPTH) ** 0.25
EPS = 1e-5
LAM_INIT = 0.8 - 0.6 * math.exp(-0.3 * 0)
LOG2E = 1.4426950408889634
SCALE_A = (D_NOPE + D_ROPE) ** -0.5 * LOG2E
SCALE_B = D_HB ** -0.5 * LOG2E

LANES = 128
VMEM_LIMIT = 56 * 1024 * 1024
TQ = 512
HQ = TQ // 2
FC = 256
N_FC = D_FF // FC
N_AUG = 6


def _dot(a, b):
    return jnp.dot(a, b, preferred_element_type=F32)


def _dot_nt(a, b):
    return lax.dot_general(a, b, (((1,), (1,)), ((), ())), preferred_element_type=F32)


def _const_spec(shape):
    nd = len(shape)
    return pl.BlockSpec(shape, lambda *_: (0,) * nd, pipeline_mode=pl.Buffered(1))


def _rms(v, g):
    return v * lax.rsqrt(jnp.mean(v * v, -1, keepdims=True) + EPS) * g


def _gelu(v):
    return 0.5 * v * (1.0 + lax.erf(v * (2.0 ** -0.5)))


def _ln(v, g, b):
    mu = jnp.mean(v, -1, keepdims=True)
    d = v - mu
    var = jnp.mean(d * d, -1, keepdims=True)
    return d * lax.rsqrt(var + EPS) * g + b


def _rope_q(qq, ct, st, qa_ref):
    for h in range(H_A):
        sl = slice(h * LANES, (h + 1) * LANES)
        qh = qq[:, sl] * ct + qq[:, H_A * LANES + h * LANES:H_A * LANES + (h + 1) * LANES] * st
        qa_ref[:, sl] = (qh * SCALE_A).astype(BF16)


def _rope_t(x, cs):
    half = D_ROPE // 2
    x1, x2, cos, sin = x[0:half], x[half:D_ROPE], cs[0:half], cs[half:D_ROPE]
    return jnp.concatenate([x1 * cos - x2 * sin, x1 * sin + x2 * cos], axis=0)


def _pre_t_kernel(x_ref, cs_ref, w1_ref, wt_ref, wqt_ref, wk_ref, wuvt_ref, gcq_ref, gckv_ref,
                  ckv_ref, krt_ref, dkt_ref, dv4_ref,
                  qat_ref, ka_ref, vat_ref, dqt_ref, dk_ref, dvt_ref):
    tm = x_ref.shape[0]
    xb = x_ref[...].astype(BF16)
    proj = _dot(xb, w1_ref[...])
    projt = _dot_nt(wt_ref[...], xb)
    cqn = _rms(proj[:, 0:D_CQ], gcq_ref[...])
    ckvn = _rms(proj[:, D_CQ:D_CQ + D_CKV], gckv_ref[...])
    ckv_ref[...] = ckvn
    dv = proj[:, 512:1024]
    for h in range(H_B):
        dv4_ref[pl.ds(h, tm, stride=H_B), :] = dv[:, h * LANES:(h + 1) * LANES]
    dvt_ref[0] = dv.T.astype(BF16)
    dqt_ref[0] = (projt[0:W_B] * SCALE_B).astype(BF16)
    dkt = projt[W_B:2 * W_B]
    dkt_ref[0] = dkt
    dk_ref[...] = dkt.T.astype(BF16)
    cs = cs_ref[...]
    r0, r1 = D_NOPE, D_NOPE + D_ROPE
    krt = _rope_t(projt[2 * W_B:2 * W_B + D_ROPE], cs)
    krt_ref[0] = krt
    krp = jnp.concatenate([jnp.zeros((r0, tm), F32), krt, jnp.zeros((LANES - r1, tm), F32)], axis=0).T
    ckvb = ckvn.astype(BF16)
    kn = _dot(ckvb, wk_ref[...])
    vat_ref[0] = _dot_nt(wuvt_ref[...], ckvb).astype(BF16)
    qqt = _dot_nt(wqt_ref[...], cqn.astype(BF16))
    zrows = jnp.zeros((LANES - r1, tm), F32)
    for h in range(H_A):
        sl = slice(h * LANES, (h + 1) * LANES)
        ka_ref[:, sl] = (kn[:, sl] + krp).astype(BF16)
        qh = jnp.concatenate([qqt[h * LANES:h * LANES + r0], _rope_t(qqt[h * LANES + r0:h * LANES + r1], cs),
                              zrows], axis=0)
        qat_ref[0, sl, :] = (qh * SCALE_A).astype(BF16)


def _pre_t_call(x2d, b, s, tabs, weights, tm):
    t = x2d.shape[0]
    nt = s // tm
    row = lambda c: pl.BlockSpec((tm, c), lambda i, j: (i * nt + j, 0))
    tabt = pl.BlockSpec((D_ROPE, tm), lambda i, j: (0, j))
    tr = lambda r: pl.BlockSpec((1, r, tm), lambda i, j: (i, 0, j))
    out_shapes = (
        jax.ShapeDtypeStruct((t, D_CKV), F32), jax.ShapeDtypeStruct((b, D_ROPE, s), F32),
        jax.ShapeDtypeStruct((b, W_B, s), F32), jax.ShapeDtypeStruct((t * H_B, LANES), F32),
        jax.ShapeDtypeStruct((b, H_A * LANES, s), BF16), jax.ShapeDtypeStruct((t, H_A * LANES), BF16),
        jax.ShapeDtypeStruct((b, H_A * D_VA, s), BF16), jax.ShapeDtypeStruct((b, W_B, s), BF16),
        jax.ShapeDtypeStruct((t, W_B), BF16), jax.ShapeDtypeStruct((b, W_B, s), BF16))
    dv4_spec = pl.BlockSpec((tm * H_B, LANES), lambda i, j: (i * nt + j, 0))
    return pl.pallas_call(
        _pre_t_kernel,
        grid=(b, nt),
        in_specs=[row(D_MODEL), tabt] + [_const_spec(w.shape) for w in weights],
        out_specs=(row(D_CKV), tr(D_ROPE), tr(W_B), dv4_spec, tr(H_A * LANES), row(H_A * LANES),
                   tr(H_A * D_VA), tr(W_B), row(W_B), tr(W_B)),
        out_shape=out_shapes,
        compiler_params=pltpu.CompilerParams(dimension_semantics=("arbitrary", "arbitrary"),
                                             vmem_limit_bytes=VMEM_LIMIT),
        name="pre_proj_prompt",
    )(x2d, tabs, *weights)


def _pre_s_kernel(x_ref, ct_ref, st_ref, w1_ref, wq_ref, gcq_ref, gckv_ref,
                  ckv_ref, krope_ref, ndk_ref, ndv_ref, qa_ref, dq_ref, dk_ref, dv_ref):
    proj = _dot(x_ref[...].astype(BF16), w1_ref[...])
    cqn = _rms(proj[:, 0:D_CQ], gcq_ref[...])
    ckv_ref[...] = _rms(proj[:, D_CQ:D_CQ + D_CKV], gckv_ref[...])
    dq = proj[:, 512:1024]
    dk = proj[:, 1024:1536]
    dv = proj[:, 1536:2048]
    ndk_ref[...] = dk
    ndv_ref[...] = dv
    dq_ref[...] = (dq * SCALE_B).astype(BF16)
    dk_ref[...] = dk.astype(BF16)
    dv_ref[...] = dv.astype(BF16)
    ct = ct_ref[...]
    st = st_ref[...]
    krp = proj[:, 2048:2048 + LANES] * ct + proj[:, 2048 + LANES:2048 + 2 * LANES] * st
    krope_ref[...] = pltpu.roll(krp, 64, 1)[:, 0:D_ROPE]
    _rope_q(_dot(cqn.astype(BF16), wq_ref[...]), ct, st, qa_ref)


def _pre_s_call(x2d, ct, st, w1, wq, gcq, gckv, tm):
    t = x2d.shape[0]
    row = lambda c: pl.BlockSpec((tm, c), lambda i: (i, 0))
    tab = pl.BlockSpec((tm, LANES), lambda i: (0, 0))
    out_shapes = (
        jax.ShapeDtypeStruct((t, D_CKV), F32), jax.ShapeDtypeStruct((t, D_ROPE), F32),
        jax.ShapeDtypeStruct((t, W_B), F32), jax.ShapeDtypeStruct((t, W_B), F32),
        jax.ShapeDtypeStruct((t, H_A * LANES), BF16), jax.ShapeDtypeStruct((t, W_B), BF16),
        jax.ShapeDtypeStruct((t, W_B), BF16), jax.ShapeDtypeStruct((t, W_B), BF16))
    return pl.pallas_call(
        _pre_s_kernel,
        grid=(t // tm,),
        in_specs=[row(D_MODEL), tab, tab, _const_spec(w1.shape), _const_spec(wq.shape),
                  _const_spec(gcq.shape), _const_spec(gckv.shape)],
        out_specs=(row(D_CKV), row(D_ROPE), row(W_B), row(W_B), row(H_A * LANES), row(W_B), row(W_B), row(W_B)),
        out_shape=out_shapes,
        compiler_params=pltpu.CompilerParams(dimension_semantics=("arbitrary",),
                                             vmem_limit_bytes=VMEM_LIMIT),
        name="pre_proj_sample",
    )(x2d, ct, st, w1, wq, gcq, gckv)


def _chunk_mask_t():
    r = lax.broadcasted_iota(jnp.int32, (HQ, HQ), 0) // CHUNK
    c = lax.broadcasted_iota(jnp.int32, (HQ, HQ), 1) // CHUNK
    return r <= c


def _rowmax(x):
    return jnp.max(x, -1, keepdims=True)


def _rowsum(x):
    return jnp.sum(x, -1, keepdims=True)


def _colmax(x):
    return jnp.max(x, 0, keepdims=True)


def _colsum(x):
    return jnp.sum(x, 0, keepdims=True)


def _scores_t(k_ref, mi, qt, r0, mask, corr):
    so = _dot(k_ref[mi, 0:r0, :], qt) if r0 > 0 else None
    s0 = _dot(k_ref[mi, r0:r0 + HQ, :], qt)
    s1 = _dot(k_ref[mi, r0 + HQ:r0 + TQ, :], qt[:, HQ:TQ])
    s0l = s0[:, 0:HQ]
    if corr is not None:
        s0l = s0l - corr
        s1 = s1 - corr
    s0 = jnp.concatenate([jnp.where(mask, s0l, -jnp.inf), s0[:, HQ:TQ]], axis=1)
    return so, s0, jnp.where(mask, s1, -jnp.inf)


def _probs_t(sc):
    so, s0, s1 = sc
    m = _colmax(s0)
    if so is not None:
        m = jnp.maximum(m, _colmax(so))
    mr = jnp.maximum(m[:, HQ:TQ], _colmax(s1))
    m = jnp.concatenate([m[:, 0:HQ], mr], axis=1)
    p0 = jnp.exp2(s0 - m)
    p1 = jnp.exp2(s1 - mr)
    l = _colsum(p0) + jnp.concatenate([jnp.zeros((1, HQ), F32), _colsum(p1)], axis=1)
    p_right = jnp.concatenate([p0[:, HQ:TQ], p1], axis=0).astype(BF16)
    p_left = p0[:, 0:HQ].astype(BF16)
    if so is None:
        return None, p_left, p_right, l
    po = jnp.exp2(so - m)
    return po.astype(BF16), p_left, p_right, l + _colsum(po)


def _weighted_values_t(pr, vt_ref, r0):
    po, p_left, p_right, l = pr
    o = jnp.concatenate([_dot(vt_ref[0, :, r0:r0 + HQ], p_left), _dot(vt_ref[0, :, r0:r0 + TQ], p_right)], axis=1)
    if po is not None:
        o = o + _dot(vt_ref[0, :, 0:r0], po)
    return o, l


def _attend_pipelined(n_tiles, k_ref, get_qt, vt_ref, mask, corr, finish, ahead):
    chains = [(qi, mi) for qi in reversed(range(n_tiles)) for mi in range(2)]
    sc, res = {}, {}

    def scores(c):
        qi, mi = chains[c]
        sc[c] = _scores_t(k_ref, mi, get_qt(qi, mi), qi * TQ, mask, corr)

    for c in range(min(ahead, len(chains))):
        scores(c)
    for c, (qi, mi) in enumerate(chains):
        pr = _probs_t(sc.pop(c))
        if c + ahead < len(chains):
            scores(c + ahead)
        res[c] = _weighted_values_t(pr, vt_ref, qi * TQ)
        if mi == 1:
            finish(qi, res.pop(c - 1), res.pop(c))


def _attn_a_kernel(k1_ref, k2_ref, qt_ref, vt_ref, o_ref, k_ref):
    s = k1_ref.shape[1]
    k_ref[0] = k1_ref[0]
    k_ref[1] = k2_ref[0]
    row = lax.broadcasted_iota(jnp.int32, (LANES, TQ), 0)

    def get_qt(qi, mi):
        return qt_ref[0, mi * LANES:(mi + 1) * LANES, qi * TQ:(qi + 1) * TQ]

    def finish(qi, r1, r2):
        ot = jnp.where(row < D_VA, r1[0] / r1[1], r2[0] / r2[1])
        o_ref[0, qi * TQ:(qi + 1) * TQ, :] = ot.T.astype(BF16)

    _attend_pipelined(s // TQ, k_ref, get_qt, vt_ref, _chunk_mask_t(), None, finish, ahead=2)


def _lambda_value(lq1_ref, lk1_ref, lq2_ref, lk2_ref):
    a = jnp.sum(lq1_ref[...] * lk1_ref[...], -1, keepdims=True)
    b = jnp.sum(lq2_ref[...] * lk2_ref[...], -1, keepdims=True)
    return jnp.exp(a) - jnp.exp(b) + LAM_INIT


def _attn_b_kernel(slope_ref, qaug_ref, kaug_ref, lq1_ref, lk1_ref, lq2_ref, lk2_ref, g_ref,
                   dk_ref, dqt_ref, vt_ref, o_ref, k_ref, qt_ref):
    s = dk_ref.shape[1]
    lane = lax.broadcasted_iota(jnp.int32, (s, LANES), 1)
    dk = dk_ref[0]
    k_ref[0] = jnp.where(lane < D_HB, dk, kaug_ref[0])
    k_ref[1] = jnp.where(lane >= D_HB, dk, kaug_ref[1])
    qt_ref[0, 0:D_HB, :] = dqt_ref[0, 0:D_HB, :]
    qt_ref[0, D_HB:LANES, :] = qaug_ref[0]
    qt_ref[1, 0:D_HB, :] = qaug_ref[0]
    qt_ref[1, D_HB:LANES, :] = dqt_ref[0, D_HB:LANES, :]
    slope2 = slope_ref[0, 0:1, 0:1] * (2.0 * LOG2E)
    lam = _lambda_value(lq1_ref, lk1_ref, lq2_ref, lk2_ref)
    ri = lax.broadcasted_iota(jnp.int32, (HQ, HQ), 0)
    ci = lax.broadcasted_iota(jnp.int32, (HQ, HQ), 1)
    corr = slope2 * jnp.maximum(ri - ci, 0).astype(F32)

    def get_qt(qi, mi):
        return qt_ref[mi, :, qi * TQ:(qi + 1) * TQ]

    def finish(qi, r1, r2):
        o = (r1[0] / r1[1] - lam * (r2[0] / r2[1])).T
        o_ref[0, qi * TQ:(qi + 1) * TQ, :] = (_rms(o, g_ref[...]) * (1.0 - LAM_INIT)).astype(BF16)

    _attend_pipelined(s // TQ, k_ref, get_qt, vt_ref, _chunk_mask_t(), corr, finish, ahead=2)


def _attn_a_call(ka, qat, vat):
    b, s, _ = ka.shape
    blk = lambda f: pl.BlockSpec((1, s, LANES), f)
    return pl.pallas_call(
        _attn_a_kernel,
        grid=(b, H_A // 2),
        in_specs=[blk(lambda i, j: (i, 0, 2 * j)), blk(lambda i, j: (i, 0, 2 * j + 1)),
                  pl.BlockSpec((1, 2 * LANES, s), lambda i, j: (i, j, 0)),
                  pl.BlockSpec((1, LANES, s), lambda i, j: (i, j, 0))],
        out_specs=blk(lambda i, j: (i, 0, j)),
        out_shape=jax.ShapeDtypeStruct((b, s, H_A * D_VA), BF16),
        scratch_shapes=[pltpu.VMEM((2, s, LANES), BF16)],
        compiler_params=pltpu.CompilerParams(dimension_semantics=("arbitrary", "arbitrary"),
                                             vmem_limit_bytes=VMEM_LIMIT),
        name="attn_mla",
    )(ka, ka, qat, vat)


def _attn_b_call(slopes, qaug, kaug, lq1, lk1, lq2, lk2, g, dk, dqt, dvt):
    b, s, _ = dk.shape
    blk = pl.BlockSpec((1, s, LANES), lambda i, j: (i, 0, j))
    blkt = pl.BlockSpec((1, LANES, s), lambda i, j: (i, j, 0))
    vec = lambda a: pl.BlockSpec(a.shape, lambda i, j: (0,) * a.ndim)
    return pl.pallas_call(
        _attn_b_kernel,
        grid=(b, H_B),
        in_specs=[pl.BlockSpec((1, 8, LANES), lambda i, j: (j, 0, 0)),
                  pl.BlockSpec((1, D_HB, s), lambda i, j: (j, 0, 0)),
                  vec(kaug), vec(lq1), vec(lk1), vec(lq2), vec(lk2), vec(g),
                  blk, blkt, blkt],
        out_specs=blk,
        out_shape=jax.ShapeDtypeStruct((b, s, W_B), BF16),
        scratch_shapes=[pltpu.VMEM((2, s, LANES), BF16), pltpu.VMEM((2, LANES, s), BF16)],
        compiler_params=pltpu.CompilerParams(dimension_semantics=("arbitrary", "arbitrary"),
                                             vmem_limit_bytes=VMEM_LIMIT),
        name="attn_diff",
    )(slopes, qaug, kaug, lq1, lk1, lq2, lk2, g, dk, dqt, dvt)


def _stack_rows(x, reps, width):
    n, c = x.shape
    t = jnp.concatenate([x] * reps, axis=0)
    rb = lax.broadcasted_iota(jnp.int32, (reps * n, c), 0) // n
    cb = lax.broadcasted_iota(jnp.int32, (reps * n, c), 1) // width
    return jnp.where(rb == cb, t, jnp.zeros_like(t))


def _softmax2(sc, sn):
    m = jnp.maximum(_rowmax(sc), _rowmax(sn))
    pc = jnp.exp2(sc - m)
    pn = jnp.exp2(sn - m)
    l = _rowsum(pc) + _rowsum(pn)
    return pc.astype(BF16), pn.astype(BF16), l


def _attn_s_kernel(past_len, qa_ref, dq_ref, cckv_ref, ckrt_ref, cdkt_ref, cdv4_ref,
                   nckv_ref, nkr_ref, ndk_ref, ndv_ref, wabs_ref, wuvb_ref,
                   lq1_ref, lk1_ref, lq2_ref, lk2_ref, g_ref, oa_ref, ob_ref):
    n = qa_ref.shape[1]
    rows = 2 * H_B * n
    qblk = _stack_rows(qa_ref[0], H_A, LANES)
    g = _dot(qblk, wabs_ref[...])
    qabs = g[:, 0:D_CKV].astype(BF16)
    qrp = g[:, D_CKV:D_CKV + D_ROPE].astype(BF16)
    cc = cckv_ref[0].astype(BF16)
    cn = nckv_ref[0].astype(BF16)
    sca = _dot_nt(qabs, cc) + _dot(qrp, ckrt_ref[0].astype(BF16))
    sna = _dot_nt(qabs, cn) + _dot_nt(qrp, nkr_ref[0].astype(BF16))
    qb = _stack_rows(dq_ref[0], 2 * H_B, D_HB)
    scb = _dot(qb, cdkt_ref[0].astype(BF16))
    snb = _dot_nt(qb, ndk_ref[0])
    pca, pna, la = _softmax2(sca, sna)
    r = lax.broadcasted_iota(jnp.int32, (rows, 1), 0)
    hh = r // (2 * n)
    slope = jnp.where(hh == 0, 2.0 ** -2, jnp.where(hh == 1, 2.0 ** -4,
                      jnp.where(hh == 2, 2.0 ** -6, 2.0 ** -8))).astype(F32) * LOG2E
    qpos = (r % n).astype(F32)
    kc = lax.broadcasted_iota(jnp.int32, (1, past_len), 1).astype(F32)
    kn = lax.broadcasted_iota(jnp.int32, (1, n), 1).astype(F32)
    scb = scb - slope * jnp.abs((qpos + float(past_len)) - kc)
    snb = snb - slope * jnp.abs(qpos - kn)
    ctx = ((_dot(pca, cc) + _dot(pna, cn)) / la).astype(BF16)
    pcb, pnb, lb = _softmax2(scb, snb)
    oa = _dot(ctx[0:n], wuvb_ref[0:D_CKV, :])
    for h in range(1, H_A):
        oa = oa + _dot(ctx[h * n:(h + 1) * n], wuvb_ref[h * D_CKV:(h + 1) * D_CKV, :])
    oa_ref[0] = oa.astype(BF16)
    lam = _lambda_value(lq1_ref, lk1_ref, lq2_ref, lk2_ref)
    inv_l = 1.0 / lb
    for h in range(H_B):
        sl = slice(h * LANES, (h + 1) * LANES)
        rs = slice(2 * h * n, (2 * h + 2) * n)
        vh = cdv4_ref[0, pl.ds(h, past_len, stride=H_B), :].astype(BF16)
        of = (_dot(pcb[rs], vh) + _dot(pnb[rs], ndv_ref[0, :, sl])) * inv_l[rs]
        o = of[0:n] - lam * of[n:2 * n]
        ob_ref[0, :, sl] = (_rms(o, g_ref[...]) * (1.0 - LAM_INIT)).astype(BF16)


def _attn_s_call(qa, dq, cckv, ckrt, cdkt, cdv4, nckv, nkr, ndk, ndv, wabs, wuvb, lq1, lk1, lq2, lk2, g):
    b, n, _ = qa.shape
    past_len = cckv.shape[1]
    per_b = lambda a: pl.BlockSpec((1,) + a.shape[1:], lambda i: (i, 0, 0))
    vec = lambda a: pl.BlockSpec(a.shape, lambda i: (0,) * a.ndim)
    args = (qa, dq, cckv, ckrt, cdkt, cdv4, nckv, nkr, ndk, ndv)
    consts = (wabs, wuvb)
    vecs = (lq1, lk1, lq2, lk2, g)
    return pl.pallas_call(
        functools.partial(_attn_s_kernel, past_len),
        grid=(b,),
        in_specs=[per_b(a) for a in args] + [_const_spec(a.shape) for a in consts] + [vec(a) for a in vecs],
        out_specs=(pl.BlockSpec((1, n, H_A * D_VA), lambda i: (i, 0, 0)),
                   pl.BlockSpec((1, n, W_B), lambda i: (i, 0, 0))),
        out_shape=(jax.ShapeDtypeStruct((b, n, H_A * D_VA), BF16), jax.ShapeDtypeStruct((b, n, W_B), BF16)),
        compiler_params=pltpu.CompilerParams(dimension_semantics=("arbitrary",),
                                             vmem_limit_bytes=VMEM_LIMIT),
        name="attn_sample",
    )(*args, *consts, *vecs)


def _post_kernel(nb, tiles_per_seq, x_ref, oa_ref, ob_ref, p_ref, state_ref,
                 woa_ref, wob_ref, ln1g_ref, ln1b_ref, wup_ref, cw_ref, cb_ref, wdn_ref,
                 ln2g_ref, ln2b_ref, wg_ref, bg_ref, wp_ref,
                 y_ref, nconv_ref, carry_ref, ush_ref, act_ref):
    tm = x_ref.shape[0]
    s = tm // nb
    i = pl.program_id(0)

    @pl.when(i % tiles_per_seq == 0)
    def _():
        carry_ref[...] = state_ref[...]

    a = _dot(oa_ref[...], woa_ref[...]) + _dot(ob_ref[...], wob_ref[...])
    h = _ln(ALPHA * x_ref[...] + a, ln1g_ref[...], ln1b_ref[...])
    hb = h.astype(BF16)

    def conv(c0):
        u = _dot(hb, wup_ref[:, c0:c0 + FC])
        u3 = u.reshape(nb, s, FC)
        tail = u3[:, s - 2:s, :]
        z = cb_ref[:, c0:c0 + FC] + u3 * cw_ref[2:3, c0:c0 + FC]
        zs = []
        for k in range(FC // LANES):
            cl = slice(c0 + k * LANES, c0 + (k + 1) * LANES)
            sl = slice(k * LANES, (k + 1) * LANES)
            for b in range(nb):
                ush_ref[k, b, pl.ds(2 * 6, 2, stride=2), :] = carry_ref[b, :, cl]
                ush_ref[k, b, pl.ds(2 * 8, s, stride=2), :] = u3[b, :, sl]
            um2 = jnp.stack([ush_ref[k, b, pl.ds(2 * 6, s, stride=2), :] for b in range(nb)])
            um1 = jnp.stack([ush_ref[k, b, pl.ds(2 * 7, s, stride=2), :] for b in range(nb)])
            zs.append(z[:, :, sl] + um2 * cw_ref[0:1, cl] + um1 * cw_ref[1:2, cl])
        carry_ref[:, :, c0:c0 + FC] = tail
        nconv_ref[:, :, c0:c0 + FC] = tail
        return jnp.concatenate(zs, axis=-1).reshape(tm, FC)

    for c in range(N_FC):
        zg = conv(c * FC)
        zv = conv(D_FF + c * FC)
        act_ref[:, c * FC:(c + 1) * FC] = (_gelu(zg) * zv).astype(BF16)

    f = _dot(act_ref[...], wdn_ref[...])
    h2 = _ln(ALPHA * h + f, ln2g_ref[...], ln2b_ref[...])
    gate = jax.nn.sigmoid(_dot(h2.astype(BF16), wg_ref[...]) + bg_ref[...])
    y_ref[...] = h2 + gate * _dot(p_ref[...].astype(BF16), wp_ref[...])


def _post_call(x2d, oa, ob, p2d, state, weights, tm, nb, tiles_per_seq):
    t = x2d.shape[0]
    n_seq = state.shape[0]
    s = tm // nb
    row = lambda c: pl.BlockSpec((tm, c), lambda i: (i, 0))
    st_spec = pl.BlockSpec((nb, CONV_W - 1, 2 * D_FF), lambda i: (i // tiles_per_seq, 0, 0))
    return pl.pallas_call(
        functools.partial(_post_kernel, nb, tiles_per_seq),
        grid=(t // tm,),
        in_specs=[row(D_MODEL), row(H_A * D_VA), row(W_B), row(D_PLE), st_spec]
                 + [_const_spec(w.shape) for w in weights],
        out_specs=(row(D_MODEL), st_spec),
        out_shape=(jax.ShapeDtypeStruct((t, D_MODEL), F32),
                   jax.ShapeDtypeStruct((n_seq, CONV_W - 1, 2 * D_FF), F32)),
        scratch_shapes=[pltpu.VMEM((nb, CONV_W - 1, 2 * D_FF), F32),
                        pltpu.VMEM((FC // LANES, nb, 2 * (s + 8), LANES), F32),
                        pltpu.VMEM((tm, D_FF), BF16)],
        compiler_params=pltpu.CompilerParams(dimension_semantics=("arbitrary",),
                                             vmem_limit_bytes=VMEM_LIMIT),
        name="post_ffn",
    )(x2d, oa, ob, p2d, state, *weights)


def _rope_angles(pos):
    half = D_ROPE // 2
    inv = 1.0 / (ROPE_BASE ** (jnp.arange(half, dtype=F32) / half))
    ang = pos.astype(F32)[:, None] * inv[None, :]
    return jnp.cos(ang), jnp.sin(ang)


def _rope_tables(pos):
    cos, sin = _rope_angles(pos)
    n = pos.shape[0]
    ct = jnp.concatenate([jnp.ones((n, D_NOPE), F32), cos, cos, jnp.zeros((n, LANES - D_NOPE - D_ROPE), F32)], 1)
    st = jnp.concatenate([jnp.zeros((n, D_NOPE), F32), sin, sin, jnp.zeros((n, LANES - D_NOPE - D_ROPE), F32)], 1)
    return ct, st


def _rope_table_t(pos):
    cos, sin = _rope_angles(pos)
    return jnp.concatenate([cos.T, sin.T], 0)


def _rot_half_cols(w):
    half = D_ROPE // 2
    return jnp.concatenate([-w[..., half:], w[..., :half]], -1)


def _prep_weights(w_in, w_uq, w_uk, w_uv):
    o1, o2, o3 = D_CQ, D_CQ + D_CKV, D_CQ + D_CKV + D_ROPE
    o4, o5 = o3 + W_B, o3 + 2 * W_B
    w_kr = w_in[:, o2:o3]
    w_dq, w_dk, w_dv = w_in[:, o3:o4], w_in[:, o4:o5], w_in[:, o5:]
    z = lambda c: jnp.zeros((D_MODEL, c), F32)
    w1p = jnp.concatenate([w_in[:, :o2], w_dv], 1).astype(BF16)
    wtp = jnp.concatenate([w_dq, w_dk, w_kr], 1).T.astype(BF16)
    w1s = jnp.concatenate([w_in[:, :o2], w_dq, w_dk, w_dv,
                           z(D_NOPE), w_kr, z(LANES - D_NOPE - D_ROPE),
                           z(D_NOPE), _rot_half_cols(w_kr), z(LANES - D_NOPE - D_ROPE)], 1).astype(BF16)
    wq3 = w_uq.reshape(D_CQ, H_A, D_NOPE + D_ROPE)
    zq = lambda c: jnp.zeros((D_CQ, H_A, c), F32)
    wq_a = jnp.concatenate([wq3, zq(LANES - D_NOPE - D_ROPE)], -1).reshape(D_CQ, H_A * LANES)
    wq_b = jnp.concatenate([zq(D_NOPE), _rot_half_cols(wq3[..., D_NOPE:]), zq(LANES - D_NOPE - D_ROPE)],
                           -1).reshape(D_CQ, H_A * LANES)
    wq = jnp.concatenate([wq_a, wq_b], 1).astype(BF16)
    wk3 = w_uk.reshape(D_CKV, H_A, D_NOPE)
    wk = jnp.concatenate([wk3, jnp.zeros((D_CKV, H_A, LANES - D_NOPE), F32)], -1).reshape(D_CKV, H_A * LANES)
    wabs_lat = jnp.concatenate([jnp.transpose(wk3, (1, 2, 0)),
                                jnp.zeros((H_A, LANES - D_NOPE, D_CKV), F32)], 1)
    pick = jnp.zeros((LANES, LANES), F32).at[D_NOPE + jnp.arange(D_ROPE), jnp.arange(D_ROPE)].set(1.0)
    wabs = jnp.concatenate([wabs_lat, jnp.broadcast_to(pick, (H_A, LANES, LANES))], -1)
    wabs = wabs.reshape(H_A * LANES, D_CKV + LANES).astype(BF16)
    wv3 = w_uv.reshape(D_CKV, H_A, D_VA)
    eye = jnp.eye(H_A, dtype=F32)
    wuvb = jnp.einsum('lhd,hg->hlgd', wv3, eye).reshape(H_A * D_CKV, H_A * D_VA).astype(BF16)
    return w1p, wtp, w1s, wq, wq_a.T.astype(BF16), wk.astype(BF16), w_uv.T.astype(BF16), wabs, wuvb


def _alibi_tables(s):
    j = jnp.arange(s, dtype=jnp.int32)
    j_hi = ((j // 256) * 256).astype(F32)
    j_lo = (j % 256).astype(F32)
    cols = jnp.stack([j_hi, j_lo] * 3, axis=1)
    kaug = jnp.zeros((2, s, LANES), F32)
    kaug = kaug.at[0, :, D_HB:D_HB + N_AUG].set(cols).at[1, :, 0:N_AUG].set(cols).astype(BF16)
    c_hi = jnp.asarray(LOG2E, F32).astype(BF16).astype(F32)
    c_mid = (jnp.asarray(LOG2E, F32) - c_hi).astype(BF16).astype(F32)
    c_lo = (jnp.asarray(LOG2E, F32) - c_hi - c_mid).astype(BF16).astype(F32)
    consts = jnp.stack([c_hi, c_hi, c_mid, c_mid, c_lo, c_lo])
    slopes = jnp.asarray([2.0 ** (-8.0 * (h + 1) / H_B) for h in range(H_B)], F32)
    vals = slopes[:, None] * consts[None, :]
    qaug = jnp.zeros((H_B, D_HB, s), F32)
    qaug = qaug.at[:, 0:N_AUG, :].set(jnp.broadcast_to(vals[:, :, None], (H_B, N_AUG, s))).astype(BF16)
    return kaug, qaug, jnp.broadcast_to(slopes[:, None, None], (H_B, 8, LANES))


def kernel(x_prompt, x_sample, cache_ckv, cache_krope, cache_diff_k, cache_diff_v, state_ffn_conv, p_prompt, p_sample, w_in, g_cq, w_uq, g_ckv, w_uk, w_uv, lambda_q1, lambda_k1, lambda_q2, lambda_k2, g_subln, w_o, ln1_g, ln1_b, w_up, conv_w, conv_b, w_down, ln2_g, ln2_b, w_ple_gate, b_ple_gate, w_ple_proj):
    b, s, _ = x_prompt.shape
    bs, n, _ = x_sample.shape
    past_len = cache_ckv.shape[2]
    w1p, wtp, w1s, wq, wqt, wk, wuvt, wabs, wuvb = _prep_weights(w_in[0], w_uq[0], w_uk[0], w_uv[0])
    post_w = (w_o[0][:H_A * D_VA].astype(BF16), w_o[0][H_A * D_VA:].astype(BF16), ln1_g, ln1_b,
              w_up[0].astype(BF16), conv_w[0], conv_b, w_down[0].astype(BF16), ln2_g, ln2_b,
              w_ple_gate[0].astype(BF16), b_ple_gate, w_ple_proj[0].astype(BF16))
    kaug, qaug, slopes = _alibi_tables(s)
    lams = (lambda_q1, lambda_k1, lambda_q2, lambda_k2)

    tm = 512
    x2p = x_prompt.reshape(b * s, D_MODEL)
    (ckv_p, krt_p, dkt_p, dv4_p, qat, ka, vat, dqt, dk, dvt) = _pre_t_call(
        x2p, b, s, _rope_table_t(jnp.arange(s)), (w1p, wtp, wqt, wk, wuvt, g_cq, g_ckv), tm)
    r3 = lambda a: a.reshape(b, s, a.shape[-1])
    oa = _attn_a_call(r3(ka), qat, vat)
    ob = _attn_b_call(slopes, qaug, kaug, *lams, g_subln, r3(dk), dqt, dvt)
    zero_state = jnp.zeros((b, CONV_W - 1, 2 * D_FF), F32)
    y_p, conv_p = _post_call(x2p, oa.reshape(b * s, -1), ob.reshape(b * s, -1),
                             p_prompt[0].reshape(b * s, D_PLE), zero_state, post_w, tm, 1, s // tm)

    nbs = 8
    tms = nbs * n
    cts, sts = _rope_tables(past_len + jnp.arange(n))
    cts, sts = jnp.tile(cts, (nbs, 1)), jnp.tile(sts, (nbs, 1))
    x2s = x_sample.reshape(bs * n, D_MODEL)
    (ckv_s, kr_s, ndk_s, ndv_s, qa_s, dq_s, dk_s, dv_s) = _pre_s_call(x2s, cts, sts, w1s, wq, g_cq, g_ckv, tms)
    q3 = lambda a: a.reshape(bs, n, a.shape[-1])
    ckrt = jnp.transpose(cache_krope[0], (0, 2, 1))
    cdkt = jnp.transpose(cache_diff_k[0], (0, 2, 3, 4, 1)).reshape(bs, W_B, past_len)
    cdv4 = cache_diff_v[0].reshape(bs, past_len * H_B, 2 * D_HB)
    oa_s, ob_s = _attn_s_call(
        q3(qa_s), q3(dq_s), cache_ckv[0], ckrt, cdkt, cdv4,
        q3(ckv_s), q3(kr_s), q3(dk_s), q3(dv_s), wabs, wuvb, *lams, g_subln)
    y_s, conv_s = _post_call(x2s, oa_s.reshape(bs * n, -1), ob_s.reshape(bs * n, -1),
                             p_sample[0].reshape(bs * n, D_PLE), state_ffn_conv[0], post_w, tms, nbs, 1)

    new_kr_p = jnp.transpose(krt_p, (0, 2, 1))[None]
    new_dk_p = jnp.transpose(dkt_p.reshape(b, H_B, 2, D_HB, s), (0, 4, 1, 2, 3))[None]
    return (y_p.reshape(b, s, D_MODEL), y_s.reshape(bs, n, D_MODEL),
            ckv_p.reshape(1, b, s, D_CKV), new_kr_p, new_dk_p,
            dv4_p.reshape(1, b, s, H_B, 2 * D_HB),
            conv_p[None],
            ckv_s.reshape(1, bs, n, D_CKV), kr_s.reshape(1, bs, n, D_ROPE),
            ndk_s.reshape(1, bs, n, H_B, 2, D_HB), ndv_s.reshape(1, bs, n, H_B, 2 * D_HB),
            conv_s[None])
```

```python
import functools
import math

import numpy as np
import jax
import jax.numpy as jnp
from jax import lax
from jax.experimental import pallas as pl
from jax.experimental.pallas import tpu as pltpu

F32 = jnp.float32
BF16 = jnp.bfloat16

D_MODEL = 1024
DEPTH = 1
CHUNK = 64
H_A = 8
D_NOPE = 64
D_ROPE = 32
D_VA = 64
D_CQ = 256
D_CKV = 256
ROPE_BASE = 10000.0
H_B = 4
D_HB = 64
W_B = H_B * 2 * D_HB
D_FF = 2816
CONV_W = 3
D_PLE = 256
ALPHA = (2 * DEPTH) ** 0.25
EPS = 1e-5
LAM_INIT = 0.8 - 0.6 * math.exp(-0.3 * 0)
LOG2E = 1.4426950408889634
SCALE_A = (D_NOPE + D_ROPE) ** -0.5 * LOG2E
SCALE_B = D_HB ** -0.5 * LOG2E

LANES = 128
VMEM_LIMIT = 56 * 1024 * 1024
TQ = 512
HQ = TQ // 2
FC = 256
N_FC = D_FF // FC
N_AUG = 6
POS_SPLIT = 256
TM = 512
SAMPLE_SEQS = 8
O_PROJ = D_CQ + D_CKV
PITCH = 2
HALO = 8


def _dot(a, b):
    return jnp.dot(a, b, preferred_element_type=F32)


def _dot_nt(a, b):
    return lax.dot_general(a, b, (((1,), (1,)), ((), ())), preferred_element_type=F32)


def _const_spec(shape):
    nd = len(shape)
    return pl.BlockSpec(shape, lambda *_: (0,) * nd, pipeline_mode=pl.Buffered(1))


def _rms(v, g):
    return v * lax.rsqrt(jnp.mean(v * v, -1, keepdims=True) + EPS) * g


def _gelu(v):
    return 0.5 * v * (1.0 + lax.erf(v * (2.0 ** -0.5)))


def _ln(v, g, b):
    mu = jnp.mean(v, -1, keepdims=True)
    d = v - mu
    var = jnp.mean(d * d, -1, keepdims=True)
    return d * lax.rsqrt(var + EPS) * g + b


def _rope_q(qq, ct, st, qa_ref):
    for h in range(H_A):
        sl = slice(h * LANES, (h + 1) * LANES)
        qh = qq[:, sl] * ct + qq[:, H_A * LANES + h * LANES:H_A * LANES + (h + 1) * LANES] * st
        qa_ref[:, sl] = (qh * SCALE_A).astype(BF16)


def _rope_t(x, cs):
    half = D_ROPE // 2
    x1, x2, cos, sin = x[0:half], x[half:D_ROPE], cs[0:half], cs[half:D_ROPE]
    return jnp.concatenate([x1 * cos - x2 * sin, x1 * sin + x2 * cos], axis=0)


def _pre_t_kernel(x_ref, cs_ref, w1_ref, wt_ref, wqt_ref, wk_ref, wuvt_ref, gcq_ref, gckv_ref,
                  ckv_ref, krt_ref, dkt_ref, dv4_ref,
                  qat_ref, ka_ref, vat_ref, dqt_ref, dk_ref, dvt_ref):
    tm = x_ref.shape[0]
    xb = x_ref[...].astype(BF16)
    proj = _dot(xb, w1_ref[...])
    projt = _dot_nt(wt_ref[...], xb)
    cqn = _rms(proj[:, 0:D_CQ], gcq_ref[...])
    ckvn = _rms(proj[:, D_CQ:D_CQ + D_CKV], gckv_ref[...])
    ckv_ref[...] = ckvn
    dv = proj[:, O_PROJ:O_PROJ + W_B]
    for h in range(H_B):
        dv4_ref[pl.ds(h, tm, stride=H_B), :] = dv[:, h * LANES:(h + 1) * LANES]
    dvt_ref[0] = dv.T.astype(BF16)
    dqt_ref[0] = (projt[0:W_B] * SCALE_B).astype(BF16)
    dkt = projt[W_B:2 * W_B]
    dkt_ref[0] = dkt
    dk_ref[...] = dkt.T.astype(BF16)
    cs = cs_ref[...]
    r0, r1 = D_NOPE, D_NOPE + D_ROPE
    krt = _rope_t(projt[2 * W_B:2 * W_B + D_ROPE], cs)
    krt_ref[0] = krt
    ckvb = ckvn.astype(BF16)
    knt = _dot_nt(wk_ref[...], ckvb)
    vat_ref[0] = _dot_nt(wuvt_ref[...], ckvb).astype(BF16)
    qqt = _dot_nt(wqt_ref[...], cqn.astype(BF16))
    zrows = jnp.zeros((LANES - r1, tm), F32)
    for h in range(H_A):
        sl = slice(h * LANES, (h + 1) * LANES)
        kht = jnp.concatenate([knt[h * D_NOPE:(h + 1) * D_NOPE], krt, zrows], axis=0)
        ka_ref[:, sl] = kht.T.astype(BF16)
        qh = jnp.concatenate([qqt[h * LANES:h * LANES + r0], _rope_t(qqt[h * LANES + r0:h * LANES + r1], cs),
                              zrows], axis=0)
        qat_ref[0, sl, :] = (qh * SCALE_A).astype(BF16)


def _pre_t_call(x2d, b, s, tabs, weights, tm):
    t = x2d.shape[0]
    nt = s // tm
    row = lambda c: pl.BlockSpec((tm, c), lambda i, j: (i * nt + j, 0))
    tabt = pl.BlockSpec((D_ROPE, tm), lambda i, j: (0, j))
    tr = lambda r: pl.BlockSpec((1, r, tm), lambda i, j: (i, 0, j))
    out_shapes = (
        jax.ShapeDtypeStruct((t, D_CKV), F32), jax.ShapeDtypeStruct((b, D_ROPE, s), F32),
        jax.ShapeDtypeStruct((b, W_B, s), F32), jax.ShapeDtypeStruct((t * H_B, LANES), F32),
        jax.ShapeDtypeStruct((b, H_A * LANES, s), BF16), jax.ShapeDtypeStruct((t, H_A * LANES), BF16),
        jax.ShapeDtypeStruct((b, H_A * D_VA, s), BF16), jax.ShapeDtypeStruct((b, W_B, s), BF16),
        jax.ShapeDtypeStruct((t, W_B), BF16), jax.ShapeDtypeStruct((b, W_B, s), BF16))
    dv4_spec = pl.BlockSpec((tm * H_B, LANES), lambda i, j: (i * nt + j, 0))
    return pl.pallas_call(
        _pre_t_kernel,
        grid=(b, nt),
        in_specs=[row(D_MODEL), tabt] + [_const_spec(w.shape) for w in weights],
        out_specs=(row(D_CKV), tr(D_ROPE), tr(W_B), dv4_spec, tr(H_A * LANES), row(H_A * LANES),
                   tr(H_A * D_VA), tr(W_B), row(W_B), tr(W_B)),
        out_shape=out_shapes,
        compiler_params=pltpu.CompilerParams(dimension_semantics=("arbitrary", "arbitrary"),
                                             vmem_limit_bytes=VMEM_LIMIT),
        name="pre_proj_prompt",
    )(x2d, tabs, *weights)


def _pre_s_kernel(x_ref, ct_ref, st_ref, w1_ref, wq_ref, gcq_ref, gckv_ref,
                  ckv_ref, krope_ref, ndk_ref, ndv_ref, qa_ref, dq_ref, dk_ref, dv_ref):
    proj = _dot(x_ref[...].astype(BF16), w1_ref[...])
    cqn = _rms(proj[:, 0:D_CQ], gcq_ref[...])
    ckv_ref[...] = _rms(proj[:, D_CQ:D_CQ + D_CKV], gckv_ref[...])
    dq = proj[:, O_PROJ:O_PROJ + W_B]
    dk = proj[:, O_PROJ + W_B:O_PROJ + 2 * W_B]
    dv = proj[:, O_PROJ + 2 * W_B:O_PROJ + 3 * W_B]
    o_kr = O_PROJ + 3 * W_B
    ndk_ref[...] = dk
    ndv_ref[...] = dv
    dq_ref[...] = (dq * SCALE_B).astype(BF16)
    dk_ref[...] = dk.astype(BF16)
    dv_ref[...] = dv.astype(BF16)
    ct = ct_ref[...]
    st = st_ref[...]
    krp = proj[:, o_kr:o_kr + LANES] * ct + proj[:, o_kr + LANES:o_kr + 2 * LANES] * st
    krope_ref[...] = pltpu.roll(krp, LANES - D_NOPE, 1)[:, 0:D_ROPE]
    _rope_q(_dot(cqn.astype(BF16), wq_ref[...]), ct, st, qa_ref)


def _pre_s_call(x2d, ct, st, w1, wq, gcq, gckv, tm):
    t = x2d.shape[0]
    row = lambda c: pl.BlockSpec((tm, c), lambda i: (i, 0))
    tab = pl.BlockSpec((tm, LANES), lambda i: (0, 0))
    out_shapes = (
        jax.ShapeDtypeStruct((t, D_CKV), F32), jax.ShapeDtypeStruct((t, D_ROPE), F32),
        jax.ShapeDtypeStruct((t, W_B), F32), jax.ShapeDtypeStruct((t, W_B), F32),
        jax.ShapeDtypeStruct((t, H_A * LANES), BF16), jax.ShapeDtypeStruct((t, W_B), BF16),
        jax.ShapeDtypeStruct((t, W_B), BF16), jax.ShapeDtypeStruct((t, W_B), BF16))
    return pl.pallas_call(
        _pre_s_kernel,
        grid=(t // tm,),
        in_specs=[row(D_MODEL), tab, tab, _const_spec(w1.shape), _const_spec(wq.shape),
                  _const_spec(gcq.shape), _const_spec(gckv.shape)],
        out_specs=(row(D_CKV), row(D_ROPE), row(W_B), row(W_B), row(H_A * LANES), row(W_B), row(W_B), row(W_B)),
        out_shape=out_shapes,
        compiler_params=pltpu.CompilerParams(dimension_semantics=("arbitrary",),
                                             vmem_limit_bytes=VMEM_LIMIT),
        name="pre_proj_sample",
    )(x2d, ct, st, w1, wq, gcq, gckv)


def _chunk_mask_t():
    r = lax.broadcasted_iota(jnp.int32, (HQ, HQ), 0) // CHUNK
    c = lax.broadcasted_iota(jnp.int32, (HQ, HQ), 1) // CHUNK
    return r <= c


def _rowmax(x):
    return jnp.max(x, -1, keepdims=True)


def _rowsum(x):
    return jnp.sum(x, -1, keepdims=True)


def _colmax(x):
    return jnp.max(x, 0, keepdims=True)


def _colsum(x):
    return jnp.sum(x, 0, keepdims=True)


def _scores_t(k_ref, mi, qt, r0, mask, corr):
    so = _dot(k_ref[mi, 0:r0, :], qt) if r0 > 0 else None
    s0 = _dot(k_ref[mi, r0:r0 + HQ, :], qt)
    s1 = _dot(k_ref[mi, r0 + HQ:r0 + TQ, :], qt[:, HQ:TQ])
    s0l = s0[:, 0:HQ]
    if corr is not None:
        s0l = s0l - corr
        s1 = s1 - corr
    s0 = jnp.concatenate([jnp.where(mask, s0l, -jnp.inf), s0[:, HQ:TQ]], axis=1)
    return so, s0, jnp.where(mask, s1, -jnp.inf)


def _probs_t(sc):
    so, s0, s1 = sc
    m = _colmax(s0)
    if so is not None:
        m = jnp.maximum(m, _colmax(so))
    mr = jnp.maximum(m[:, HQ:TQ], _colmax(s1))
    m = jnp.concatenate([m[:, 0:HQ], mr], axis=1)
    p0 = jnp.exp2(s0 - m)
    p1 = jnp.exp2(s1 - mr)
    l = _colsum(p0) + jnp.concatenate([jnp.zeros((1, HQ), F32), _colsum(p1)], axis=1)
    p_right = jnp.concatenate([p0[:, HQ:TQ], p1], axis=0).astype(BF16)
    p_left = p0[:, 0:HQ].astype(BF16)
    if so is None:
        return None, p_left, p_right, l
    po = jnp.exp2(so - m)
    return po.astype(BF16), p_left, p_right, l + _colsum(po)


def _weighted_values_t(pr, vt_ref, r0):
    po, p_left, p_right, l = pr
    o = jnp.concatenate([_dot(vt_ref[0, :, r0:r0 + HQ], p_left), _dot(vt_ref[0, :, r0:r0 + TQ], p_right)], axis=1)
    if po is not None:
        o = o + _dot(vt_ref[0, :, 0:r0], po)
    return o, l


def _attend_pipelined(n_tiles, k_ref, get_qt, vt_ref, mask, corr, finish, ahead):
    chains = [(qi, mi) for qi in reversed(range(n_tiles)) for mi in range(2)]
    sc, res = {}, {}

    def scores(c):
        qi, mi = chains[c]
        sc[c] = _scores_t(k_ref, mi, get_qt(qi, mi), qi * TQ, mask, corr)

    for c in range(min(ahead, len(chains))):
        scores(c)
    for c, (qi, mi) in enumerate(chains):
        pr = _probs_t(sc.pop(c))
        if c + ahead < len(chains):
            scores(c + ahead)
        res[c] = _weighted_values_t(pr, vt_ref, qi * TQ)
        if mi == 1:
            finish(qi, res.pop(c - 1), res.pop(c))


def _attn_a_kernel(k1_ref, k2_ref, qt_ref, vt_ref, o_ref, k_ref):
    s = k1_ref.shape[1]
    k_ref[0] = k1_ref[0]
    k_ref[1] = k2_ref[0]
    row = lax.broadcasted_iota(jnp.int32, (LANES, TQ), 0)

    def get_qt(qi, mi):
        return qt_ref[0, mi * LANES:(mi + 1) * LANES, qi * TQ:(qi + 1) * TQ]

    def finish(qi, r1, r2):
        ot = jnp.where(row < D_VA, r1[0] / r1[1], r2[0] / r2[1])
        o_ref[0, qi * TQ:(qi + 1) * TQ, :] = ot.T.astype(BF16)

    _attend_pipelined(s // TQ, k_ref, get_qt, vt_ref, _chunk_mask_t(), None, finish, ahead=2)


def _lambda_value(lq1_ref, lk1_ref, lq2_ref, lk2_ref):
    a = jnp.sum(lq1_ref[...] * lk1_ref[...], -1, keepdims=True)
    b = jnp.sum(lq2_ref[...] * lk2_ref[...], -1, keepdims=True)
    return jnp.exp(a) - jnp.exp(b) + LAM_INIT


def _attn_b_kernel(slope_ref, qaug_ref, kaug_ref, lq1_ref, lk1_ref, lq2_ref, lk2_ref, g_ref,
                   dk_ref, dqt_ref, vt_ref, o_ref, k_ref, qt_ref):
    s = dk_ref.shape[1]
    lane = lax.broadcasted_iota(jnp.int32, (s, LANES), 1)
    dk = dk_ref[0]
    k_ref[0] = jnp.where(lane < D_HB, dk, kaug_ref[0])
    k_ref[1] = jnp.where(lane >= D_HB, dk, kaug_ref[1])
    qt_ref[0, 0:D_HB, :] = dqt_ref[0, 0:D_HB, :]
    qt_ref[0, D_HB:LANES, :] = qaug_ref[0]
    qt_ref[1, 0:D_HB, :] = qaug_ref[0]
    qt_ref[1, D_HB:LANES, :] = dqt_ref[0, D_HB:LANES, :]
    slope2 = slope_ref[0, 0:1, 0:1] * (2.0 * LOG2E)
    lam = _lambda_value(lq1_ref, lk1_ref, lq2_ref, lk2_ref)
    ri = lax.broadcasted_iota(jnp.int32, (HQ, HQ), 0)
    ci = lax.broadcasted_iota(jnp.int32, (HQ, HQ), 1)
    corr = slope2 * jnp.maximum(ri - ci, 0).astype(F32)

    def get_qt(qi, mi):
        return qt_ref[mi, :, qi * TQ:(qi + 1) * TQ]

    def finish(qi, r1, r2):
        o = (r1[0] / r1[1] - lam * (r2[0] / r2[1])).T
        o_ref[0, qi * TQ:(qi + 1) * TQ, :] = (_rms(o, g_ref[...]) * (1.0 - LAM_INIT)).astype(BF16)

    _attend_pipelined(s // TQ, k_ref, get_qt, vt_ref, _chunk_mask_t(), corr, finish, ahead=2)


def _attn_a_call(ka, qat, vat):
    b, s, _ = ka.shape
    blk = lambda f: pl.BlockSpec((1, s, LANES), f)
    return pl.pallas_call(
        _attn_a_kernel,
        grid=(b, H_A // 2),
        in_specs=[blk(lambda i, j: (i, 0, 2 * j)), blk(lambda i, j: (i, 0, 2 * j + 1)),
                  pl.BlockSpec((1, 2 * LANES, s), lambda i, j: (i, j, 0)),
                  pl.BlockSpec((1, LANES, s), lambda i, j: (i, j, 0))],
        out_specs=blk(lambda i, j: (i, 0, j)),
        out_shape=jax.ShapeDtypeStruct((b, s, H_A * D_VA), BF16),
        scratch_shapes=[pltpu.VMEM((2, s, LANES), BF16)],
        compiler_params=pltpu.CompilerParams(dimension_semantics=("arbitrary", "arbitrary"),
                                             vmem_limit_bytes=VMEM_LIMIT),
        name="attn_mla",
    )(ka, ka, qat, vat)


def _attn_b_call(slopes, qaug, kaug, lq1, lk1, lq2, lk2, g, dk, dqt, dvt):
    b, s, _ = dk.shape
    blk = pl.BlockSpec((1, s, LANES), lambda i, j: (i, 0, j))
    blkt = pl.BlockSpec((1, LANES, s), lambda i, j: (i, j, 0))
    vec = lambda a: pl.BlockSpec(a.shape, lambda i, j: (0,) * a.ndim)
    return pl.pallas_call(
        _attn_b_kernel,
        grid=(b, H_B),
        in_specs=[pl.BlockSpec((1, 8, LANES), lambda i, j: (j, 0, 0)),
                  pl.BlockSpec((1, D_HB, s), lambda i, j: (j, 0, 0)),
                  vec(kaug), vec(lq1), vec(lk1), vec(lq2), vec(lk2), vec(g),
                  blk, blkt, blkt],
        out_specs=blk,
        out_shape=jax.ShapeDtypeStruct((b, s, W_B), BF16),
        scratch_shapes=[pltpu.VMEM((2, s, LANES), BF16), pltpu.VMEM((2, LANES, s), BF16)],
        compiler_params=pltpu.CompilerParams(dimension_semantics=("arbitrary", "arbitrary"),
                                             vmem_limit_bytes=VMEM_LIMIT),
        name="attn_diff",
    )(slopes, qaug, kaug, lq1, lk1, lq2, lk2, g, dk, dqt, dvt)


def _alibi_slope(h):
    return 2.0 ** (-8.0 * (h + 1) / H_B)


def _stack_rows(x, reps, width):
    n, c = x.shape
    t = jnp.concatenate([x] * reps, axis=0)
    rb = lax.broadcasted_iota(jnp.int32, (reps * n, c), 0) // n
    cb = lax.broadcasted_iota(jnp.int32, (reps * n, c), 1) // width
    return jnp.where(rb == cb, t, jnp.zeros_like(t))


def _softmax2(sc, sn):
    m = jnp.maximum(_rowmax(sc), _rowmax(sn))
    pc = jnp.exp2(sc - m)
    pn = jnp.exp2(sn - m)
    l = _rowsum(pc) + _rowsum(pn)
    return pc.astype(BF16), pn.astype(BF16), l


def _attn_s_kernel(past_len, qa_ref, dq_ref, cckv_ref, ckrt_ref, cdkt_ref, cdv4_ref,
                   nckv_ref, nkr_ref, ndk_ref, ndv_ref, wabs_ref, wuvb_ref,
                   lq1_ref, lk1_ref, lq2_ref, lk2_ref, g_ref, oa_ref, ob_ref):
    n = qa_ref.shape[1]
    rows = 2 * H_B * n
    qblk = _stack_rows(qa_ref[0], H_A, LANES)
    g = _dot(qblk, wabs_ref[...])
    qabs = g[:, 0:D_CKV].astype(BF16)
    qrp = g[:, D_CKV:D_CKV + D_ROPE].astype(BF16)
    cc = cckv_ref[0].astype(BF16)
    cn = nckv_ref[0].astype(BF16)
    sca = _dot_nt(qabs, cc) + _dot(qrp, ckrt_ref[0].astype(BF16))
    sna = _dot_nt(qabs, cn) + _dot_nt(qrp, nkr_ref[0].astype(BF16))
    qb = _stack_rows(dq_ref[0], 2 * H_B, D_HB)
    scb = _dot(qb, cdkt_ref[0].astype(BF16))
    snb = _dot_nt(qb, ndk_ref[0])
    pca, pna, la = _softmax2(sca, sna)
    r = lax.broadcasted_iota(jnp.int32, (rows, 1), 0)
    hh = r // (2 * n)
    slope = jnp.zeros((rows, 1), F32)
    for h in range(H_B):
        slope = jnp.where(hh == h, _alibi_slope(h) * LOG2E, slope)
    qpos = (r % n).astype(F32)
    kc = lax.broadcasted_iota(jnp.int32, (1, past_len), 1).astype(F32)
    kn = lax.broadcasted_iota(jnp.int32, (1, n), 1).astype(F32)
    scb = scb - slope * jnp.abs((qpos + float(past_len)) - kc)
    snb = snb - slope * jnp.abs(qpos - kn)
    ctx = ((_dot(pca, cc) + _dot(pna, cn)) / la).astype(BF16)
    pcb, pnb, lb = _softmax2(scb, snb)
    oa = _dot(ctx[0:n], wuvb_ref[0:D_CKV, :])
    for h in range(1, H_A):
        oa = oa + _dot(ctx[h * n:(h + 1) * n], wuvb_ref[h * D_CKV:(h + 1) * D_CKV, :])
    oa_ref[0] = oa.astype(BF16)
    lam = _lambda_value(lq1_ref, lk1_ref, lq2_ref, lk2_ref)
    inv_l = 1.0 / lb
    for h in range(H_B):
        sl = slice(h * LANES, (h + 1) * LANES)
        rs = slice(2 * h * n, (2 * h + 2) * n)
        vh = cdv4_ref[0, pl.ds(h, past_len, stride=H_B), :].astype(BF16)
        of = (_dot(pcb[rs], vh) + _dot(pnb[rs], ndv_ref[0, :, sl])) * inv_l[rs]
        o = of[0:n] - lam * of[n:2 * n]
        ob_ref[0, :, sl] = (_rms(o, g_ref[...]) * (1.0 - LAM_INIT)).astype(BF16)


def _attn_s_call(qa, dq, cckv, ckrt, cdkt, cdv4, nckv, nkr, ndk, ndv, wabs, wuvb, lq1, lk1, lq2, lk2, g):
    b, n, _ = qa.shape
    past_len = cckv.shape[1]
    per_b = lambda a: pl.BlockSpec((1,) + a.shape[1:], lambda i: (i, 0, 0))
    vec = lambda a: pl.BlockSpec(a.shape, lambda i: (0,) * a.ndim)
    args = (qa, dq, cckv, ckrt, cdkt, cdv4, nckv, nkr, ndk, ndv)
    consts = (wabs, wuvb)
    vecs = (lq1, lk1, lq2, lk2, g)
    return pl.pallas_call(
        functools.partial(_attn_s_kernel, past_len),
        grid=(b,),
        in_specs=[per_b(a) for a in args] + [_const_spec(a.shape) for a in consts] + [vec(a) for a in vecs],
        out_specs=(pl.BlockSpec((1, n, H_A * D_VA), lambda i: (i, 0, 0)),
                   pl.BlockSpec((1, n, W_B), lambda i: (i, 0, 0))),
        out_shape=(jax.ShapeDtypeStruct((b, n, H_A * D_VA), BF16), jax.ShapeDtypeStruct((b, n, W_B), BF16)),
        compiler_params=pltpu.CompilerParams(dimension_semantics=("arbitrary",),
                                             vmem_limit_bytes=VMEM_LIMIT),
        name="attn_sample",
    )(*args, *consts, *vecs)


def _post_kernel(nb, tiles_per_seq, x_ref, oa_ref, ob_ref, p_ref, state_ref,
                 woa_ref, wob_ref, ln1g_ref, ln1b_ref, wup_ref, cw_ref, cb_ref, wdn_ref,
                 ln2g_ref, ln2b_ref, wg_ref, bg_ref, wp_ref,
                 y_ref, nconv_ref, carry_ref, ush_ref, act_ref):
    tm = x_ref.shape[0]
    s = tm // nb
    i = pl.program_id(0)

    @pl.when(i % tiles_per_seq == 0)
    def _():
        carry_ref[...] = state_ref[...]

    a = _dot(oa_ref[...], woa_ref[...]) + _dot(ob_ref[...], wob_ref[...])
    h = _ln(ALPHA * x_ref[...] + a, ln1g_ref[...], ln1b_ref[...])
    hb = h.astype(BF16)

    def conv(c0):
        u = _dot(hb, wup_ref[:, c0:c0 + FC])
        u3 = u.reshape(nb, s, FC)
        tail = u3[:, s - 2:s, :]
        z = cb_ref[:, c0:c0 + FC] + u3 * cw_ref[2:3, c0:c0 + FC]
        zs = []
        for k in range(FC // LANES):
            cl = slice(c0 + k * LANES, c0 + (k + 1) * LANES)
            sl = slice(k * LANES, (k + 1) * LANES)
            for b in range(nb):
                ush_ref[k, b, pl.ds(PITCH * (HALO - 2), 2, stride=PITCH), :] = carry_ref[b, :, cl]
                ush_ref[k, b, pl.ds(PITCH * HALO, s, stride=PITCH), :] = u3[b, :, sl]
            um2 = jnp.stack([ush_ref[k, b, pl.ds(PITCH * (HALO - 2), s, stride=PITCH), :] for b in range(nb)])
            um1 = jnp.stack([ush_ref[k, b, pl.ds(PITCH * (HALO - 1), s, stride=PITCH), :] for b in range(nb)])
            zs.append(z[:, :, sl] + um2 * cw_ref[0:1, cl] + um1 * cw_ref[1:2, cl])
        carry_ref[:, :, c0:c0 + FC] = tail
        nconv_ref[:, :, c0:c0 + FC] = tail
        return jnp.concatenate(zs, axis=-1).reshape(tm, FC)

    for c in range(N_FC):
        zg = conv(c * FC)
        zv = conv(D_FF + c * FC)
        act_ref[:, c * FC:(c + 1) * FC] = (_gelu(zg) * zv).astype(BF16)

    f = _dot(act_ref[...], wdn_ref[...])
    h2 = _ln(ALPHA * h + f, ln2g_ref[...], ln2b_ref[...])
    gate = jax.nn.sigmoid(_dot(h2.astype(BF16), wg_ref[...]) + bg_ref[...])
    y_ref[...] = h2 + gate * _dot(p_ref[...].astype(BF16), wp_ref[...])


def _post_call(x2d, oa, ob, p2d, state, weights, tm, nb, tiles_per_seq):
    t = x2d.shape[0]
    n_seq = state.shape[0]
    s = tm // nb
    row = lambda c: pl.BlockSpec((tm, c), lambda i: (i, 0))
    st_spec = pl.BlockSpec((nb, CONV_W - 1, 2 * D_FF), lambda i: (i // tiles_per_seq, 0, 0))
    return pl.pallas_call(
        functools.partial(_post_kernel, nb, tiles_per_seq),
        grid=(t // tm,),
        in_specs=[row(D_MODEL), row(H_A * D_VA), row(W_B), row(D_PLE), st_spec]
                 + [_const_spec(w.shape) for w in weights],
        out_specs=(row(D_MODEL), st_spec),
        out_shape=(jax.ShapeDtypeStruct((t, D_MODEL), F32),
                   jax.ShapeDtypeStruct((n_seq, CONV_W - 1, 2 * D_FF), F32)),
        scratch_shapes=[pltpu.VMEM((nb, CONV_W - 1, 2 * D_FF), F32),
                        pltpu.VMEM((FC // LANES, nb, PITCH * (s + HALO), LANES), F32),
                        pltpu.VMEM((tm, D_FF), BF16)],
        compiler_params=pltpu.CompilerParams(dimension_semantics=("arbitrary",),
                                             vmem_limit_bytes=VMEM_LIMIT),
        name="post_ffn",
    )(x2d, oa, ob, p2d, state, *weights)


def _rope_angles(pos):
    half = D_ROPE // 2
    inv = 1.0 / (ROPE_BASE ** (jnp.arange(half, dtype=F32) / half))
    ang = pos.astype(F32)[:, None] * inv[None, :]
    return jnp.cos(ang), jnp.sin(ang)


def _rope_tables(pos):
    cos, sin = _rope_angles(pos)
    n = pos.shape[0]
    ct = jnp.concatenate([jnp.ones((n, D_NOPE), F32), cos, cos, jnp.zeros((n, LANES - D_NOPE - D_ROPE), F32)], 1)
    st = jnp.concatenate([jnp.zeros((n, D_NOPE), F32), sin, sin, jnp.zeros((n, LANES - D_NOPE - D_ROPE), F32)], 1)
    return ct, st


def _rope_table_t(pos):
    cos, sin = _rope_angles(pos)
    return jnp.concatenate([cos.T, sin.T], 0)


def _rot_half_cols(w):
    half = D_ROPE // 2
    return jnp.concatenate([-w[..., half:], w[..., :half]], -1)


def _prep_weights(w_in, w_uq, w_uk, w_uv):
    o1, o2, o3 = D_CQ, D_CQ + D_CKV, D_CQ + D_CKV + D_ROPE
    o4, o5 = o3 + W_B, o3 + 2 * W_B
    w_kr = w_in[:, o2:o3]
    w_dq, w_dk, w_dv = w_in[:, o3:o4], w_in[:, o4:o5], w_in[:, o5:]
    z = lambda c: jnp.zeros((D_MODEL, c), F32)
    w1p = jnp.concatenate([w_in[:, :o2], w_dv], 1).astype(BF16)
    wtp = jnp.concatenate([w_dq, w_dk, w_kr], 1).T.astype(BF16)
    w1s = jnp.concatenate([w_in[:, :o2], w_dq, w_dk, w_dv,
                           z(D_NOPE), w_kr, z(LANES - D_NOPE - D_ROPE),
                           z(D_NOPE), _rot_half_cols(w_kr), z(LANES - D_NOPE - D_ROPE)], 1).astype(BF16)
    wq3 = w_uq.reshape(D_CQ, H_A, D_NOPE + D_ROPE)
    zq = lambda c: jnp.zeros((D_CQ, H_A, c), F32)
    wq_a = jnp.concatenate([wq3, zq(LANES - D_NOPE - D_ROPE)], -1).reshape(D_CQ, H_A * LANES)
    wq_b = jnp.concatenate([zq(D_NOPE), _rot_half_cols(wq3[..., D_NOPE:]), zq(LANES - D_NOPE - D_ROPE)],
                           -1).reshape(D_CQ, H_A * LANES)
    wq = jnp.concatenate([wq_a, wq_b], 1).astype(BF16)
    wk3 = w_uk.reshape(D_CKV, H_A, D_NOPE)
    wabs_lat = jnp.concatenate([jnp.transpose(wk3, (1, 2, 0)),
                                jnp.zeros((H_A, LANES - D_NOPE, D_CKV), F32)], 1)
    pick = jnp.zeros((LANES, LANES), F32).at[D_NOPE + jnp.arange(D_ROPE), jnp.arange(D_ROPE)].set(1.0)
    wabs = jnp.concatenate([wabs_lat, jnp.broadcast_to(pick, (H_A, LANES, LANES))], -1)
    wabs = wabs.reshape(H_A * LANES, D_CKV + LANES).astype(BF16)
    wv3 = w_uv.reshape(D_CKV, H_A, D_VA)
    eye = jnp.eye(H_A, dtype=F32)
    wuvb = jnp.einsum('lhd,hg->hlgd', wv3, eye).reshape(H_A * D_CKV, H_A * D_VA).astype(BF16)
    return w1p, wtp, w1s, wq, wq_a.T.astype(BF16), w_uk.T.astype(BF16), w_uv.T.astype(BF16), wabs, wuvb


def _alibi_tables(s):
    j = jnp.arange(s, dtype=jnp.int32)
    j_hi = ((j // POS_SPLIT) * POS_SPLIT).astype(F32)
    j_lo = (j % POS_SPLIT).astype(F32)
    cols = jnp.stack([j_hi, j_lo] * 3, axis=1)
    kaug = jnp.zeros((2, s, LANES), F32)
    kaug = kaug.at[0, :, D_HB:D_HB + N_AUG].set(cols).at[1, :, 0:N_AUG].set(cols).astype(BF16)
    c_hi = jnp.asarray(LOG2E, F32).astype(BF16).astype(F32)
    c_mid = (jnp.asarray(LOG2E, F32) - c_hi).astype(BF16).astype(F32)
    c_lo = (jnp.asarray(LOG2E, F32) - c_hi - c_mid).astype(BF16).astype(F32)
    consts = jnp.stack([c_hi, c_hi, c_mid, c_mid, c_lo, c_lo])
    slopes = jnp.asarray([_alibi_slope(h) for h in range(H_B)], F32)
    vals = slopes[:, None] * consts[None, :]
    qaug = jnp.zeros((H_B, D_HB, s), F32)
    qaug = qaug.at[:, 0:N_AUG, :].set(jnp.broadcast_to(vals[:, :, None], (H_B, N_AUG, s))).astype(BF16)
    return kaug, qaug, jnp.broadcast_to(slopes[:, None, None], (H_B, 8, LANES))


def kernel(x_prompt, x_sample, cache_ckv, cache_krope, cache_diff_k, cache_diff_v, state_ffn_conv, p_prompt, p_sample, w_in, g_cq, w_uq, g_ckv, w_uk, w_uv, lambda_q1, lambda_k1, lambda_q2, lambda_k2, g_subln, w_o, ln1_g, ln1_b, w_up, conv_w, conv_b, w_down, ln2_g, ln2_b, w_ple_gate, b_ple_gate, w_ple_proj):
    b, s, _ = x_prompt.shape
    bs, n, _ = x_sample.shape
    past_len = cache_ckv.shape[2]
    w1p, wtp, w1s, wq, wqt, wk, wuvt, wabs, wuvb = _prep_weights(w_in[0], w_uq[0], w_uk[0], w_uv[0])
    post_w = (w_o[0][:H_A * D_VA].astype(BF16), w_o[0][H_A * D_VA:].astype(BF16), ln1_g, ln1_b,
              w_up[0].astype(BF16), conv_w[0], conv_b, w_down[0].astype(BF16), ln2_g, ln2_b,
              w_ple_gate[0].astype(BF16), b_ple_gate, w_ple_proj[0].astype(BF16))
    kaug, qaug, slopes = _alibi_tables(s)
    lams = (lambda_q1, lambda_k1, lambda_q2, lambda_k2)

    x2p = x_prompt.reshape(b * s, D_MODEL)
    (ckv_p, krt_p, dkt_p, dv4_p, qat, ka, vat, dqt, dk, dvt) = _pre_t_call(
        x2p, b, s, _rope_table_t(jnp.arange(s)), (w1p, wtp, wqt, wk, wuvt, g_cq, g_ckv), TM)
    r3 = lambda a: a.reshape(b, s, a.shape[-1])
    oa = _attn_a_call(r3(ka), qat, vat)
    ob = _attn_b_call(slopes, qaug, kaug, *lams, g_subln, r3(dk), dqt, dvt)
    zero_state = jnp.zeros((b, CONV_W - 1, 2 * D_FF), F32)
    y_p, conv_p = _post_call(x2p, oa.reshape(b * s, -1), ob.reshape(b * s, -1),
                             p_prompt[0].reshape(b * s, D_PLE), zero_state, post_w, TM, 1, s // TM)

    nbs = SAMPLE_SEQS
    tms = nbs * n
    cts, sts = _rope_tables(past_len + jnp.arange(n))
    cts, sts = jnp.tile(cts, (nbs, 1)), jnp.tile(sts, (nbs, 1))
    x2s = x_sample.reshape(bs * n, D_MODEL)
    (ckv_s, kr_s, ndk_s, ndv_s, qa_s, dq_s, dk_s, dv_s) = _pre_s_call(x2s, cts, sts, w1s, wq, g_cq, g_ckv, tms)
    q3 = lambda a: a.reshape(bs, n, a.shape[-1])
    ckrt = jnp.transpose(cache_krope[0], (0, 2, 1))
    cdkt = jnp.transpose(cache_diff_k[0], (0, 2, 3, 4, 1)).reshape(bs, W_B, past_len)
    cdv4 = cache_diff_v[0].reshape(bs, past_len * H_B, 2 * D_HB)
    oa_s, ob_s = _attn_s_call(
        q3(qa_s), q3(dq_s), cache_ckv[0], ckrt, cdkt, cdv4,
        q3(ckv_s), q3(kr_s), q3(dk_s), q3(dv_s), wabs, wuvb, *lams, g_subln)
    y_s, conv_s = _post_call(x2s, oa_s.reshape(bs * n, -1), ob_s.reshape(bs * n, -1),
                             p_sample[0].reshape(bs * n, D_PLE), state_ffn_conv[0], post_w, tms, nbs, 1)

    new_kr_p = jnp.transpose(krt_p, (0, 2, 1))[None]
    new_dk_p = jnp.transpose(dkt_p.reshape(b, H_B, 2, D_HB, s), (0, 4, 1, 2, 3))[None]
    return (y_p.reshape(b, s, D_MODEL), y_s.reshape(bs, n, D_MODEL),
            ckv_p.reshape(1, b, s, D_CKV), new_kr_p, new_dk_p,
            dv4_p.reshape(1, b, s, H_B, 2 * D_HB),
            conv_p[None],
            ckv_s.reshape(1, bs, n, D_CKV), kr_s.reshape(1, bs, n, D_ROPE),
            ndk_s.reshape(1, bs, n, H_B, 2, D_HB), ndv_s.reshape(1, bs, n, H_B, 2 * D_HB),
            conv_s[None])
```

```python
import functools
import math

import jax
import jax.numpy as jnp
from jax import lax
from jax.experimental import pallas as pl
from jax.experimental.pallas import tpu as pltpu

F32 = jnp.float32
BF16 = jnp.bfloat16

D_MODEL = 1024
DEPTH = 1
CHUNK = 64
H_A = 8
D_NOPE = 64
D_ROPE = 32
D_VA = 64
D_CQ = 256
D_CKV = 256
ROPE_BASE = 10000.0
H_B = 4
D_HB = 64
W_B = H_B * 2 * D_HB
D_FF = 2816
CONV_W = 3
D_PLE = 256
ALPHA = (2 * DEPTH) ** 0.25
EPS = 1e-5
LAM_INIT = 0.8 - 0.6 * math.exp(-0.3 * 0)
LOG2E = 1.4426950408889634
SCALE_A = (D_NOPE + D_ROPE) ** -0.5 * LOG2E
SCALE_B = D_HB ** -0.5 * LOG2E

LANES = 128
VMEM_LIMIT = 56 * 1024 * 1024
TQ = 512
HQ = TQ // 2
FC = 256
N_FC = D_FF // FC
N_AUG = 6
POS_SPLIT = 256
TM = 512
SAMPLE_SEQS = 8
O_PROJ = D_CQ + D_CKV
PITCH = 2
HALO = 8


def _dot(a, b):
    return jnp.dot(a, b, preferred_element_type=F32)


def _dot_nt(a, b):
    return lax.dot_general(a, b, (((1,), (1,)), ((), ())), preferred_element_type=F32)


def _const_spec(shape):
    nd = len(shape)
    return pl.BlockSpec(shape, lambda *_: (0,) * nd, pipeline_mode=pl.Buffered(1))


def _rms(v, g):
    return v * lax.rsqrt(jnp.mean(v * v, -1, keepdims=True) + EPS) * g


def _gelu(v):
    return 0.5 * v * (1.0 + lax.erf(v * (2.0 ** -0.5)))


def _ln(v, g, b):
    mu = jnp.mean(v, -1, keepdims=True)
    d = v - mu
    var = jnp.mean(d * d, -1, keepdims=True)
    return d * lax.rsqrt(var + EPS) * g + b


def _rope_q(qq, ct, st, qa_ref):
    for h in range(H_A):
        sl = slice(h * LANES, (h + 1) * LANES)
        qh = qq[:, sl] * ct + qq[:, H_A * LANES + h * LANES:H_A * LANES + (h + 1) * LANES] * st
        qa_ref[:, sl] = (qh * SCALE_A).astype(BF16)


def _rope_t(x, cs):
    half = D_ROPE // 2
    x1, x2, cos, sin = x[0:half], x[half:D_ROPE], cs[0:half], cs[half:D_ROPE]
    return jnp.concatenate([x1 * cos - x2 * sin, x1 * sin + x2 * cos], axis=0)


def _pre_t_kernel(x_ref, cs_ref, w1_ref, wt_ref, wqt_ref, wk_ref, wuvt_ref, gcq_ref, gckv_ref,
                  ckv_ref, krt_ref, dkt_ref, dv4_ref,
                  qat_ref, ka_ref, vat_ref, dqt_ref, dk_ref, dvt_ref):
    tm = x_ref.shape[0]
    xb = x_ref[...].astype(BF16)
    proj = _dot(xb, w1_ref[...])
    projt = _dot_nt(wt_ref[...], xb)
    cqn = _rms(proj[:, 0:D_CQ], gcq_ref[...])
    ckvn = _rms(proj[:, D_CQ:D_CQ + D_CKV], gckv_ref[...])
    ckv_ref[...] = ckvn
    dv = proj[:, O_PROJ:O_PROJ + W_B]
    for h in range(H_B):
        dv4_ref[pl.ds(h, tm, stride=H_B), :] = dv[:, h * LANES:(h + 1) * LANES]
    dvt_ref[0] = dv.T.astype(BF16)
    dqt_ref[0] = (projt[0:W_B] * SCALE_B).astype(BF16)
    dkt = projt[W_B:2 * W_B]
    dkt_ref[0] = dkt
    dk_ref[...] = dkt.T.astype(BF16)
    cs = cs_ref[...]
    r0, r1 = D_NOPE, D_NOPE + D_ROPE
    krt = _rope_t(projt[2 * W_B:2 * W_B + D_ROPE], cs)
    krt_ref[0] = krt
    ckvb = ckvn.astype(BF16)
    knt = _dot_nt(wk_ref[...], ckvb)
    vat_ref[0] = _dot_nt(wuvt_ref[...], ckvb).astype(BF16)
    qqt = _dot_nt(wqt_ref[...], cqn.astype(BF16))
    zrows = jnp.zeros((LANES - r1, tm), F32)
    for h in range(H_A):
        sl = slice(h * LANES, (h + 1) * LANES)
        kht = jnp.concatenate([knt[h * D_NOPE:(h + 1) * D_NOPE], krt, zrows], axis=0)
        ka_ref[:, sl] = kht.T.astype(BF16)
        qh = jnp.concatenate([qqt[h * LANES:h * LANES + r0], _rope_t(qqt[h * LANES + r0:h * LANES + r1], cs),
                              zrows], axis=0)
        qat_ref[0, sl, :] = (qh * SCALE_A).astype(BF16)


def _pre_t_call(x2d, b, s, tabs, weights, tm):
    t = x2d.shape[0]
    nt = s // tm
    row = lambda c: pl.BlockSpec((tm, c), lambda i, j: (i * nt + j, 0))
    tabt = pl.BlockSpec((D_ROPE, tm), lambda i, j: (0, j))
    tr = lambda r: pl.BlockSpec((1, r, tm), lambda i, j: (i, 0, j))
    out_shapes = (
        jax.ShapeDtypeStruct((t, D_CKV), F32), jax.ShapeDtypeStruct((b, D_ROPE, s), F32),
        jax.ShapeDtypeStruct((b, W_B, s), F32), jax.ShapeDtypeStruct((t * H_B, LANES), F32),
        jax.ShapeDtypeStruct((b, H_A * LANES, s), BF16), jax.ShapeDtypeStruct((t, H_A * LANES), BF16),
        jax.ShapeDtypeStruct((b, H_A * D_VA, s), BF16), jax.ShapeDtypeStruct((b, W_B, s), BF16),
        jax.ShapeDtypeStruct((t, W_B), BF16), jax.ShapeDtypeStruct((b, W_B, s), BF16))
    dv4_spec = pl.BlockSpec((tm * H_B, LANES), lambda i, j: (i * nt + j, 0))
    return pl.pallas_call(
        _pre_t_kernel,
        grid=(b, nt),
        in_specs=[row(D_MODEL), tabt] + [_const_spec(w.shape) for w in weights],
        out_specs=(row(D_CKV), tr(D_ROPE), tr(W_B), dv4_spec, tr(H_A * LANES), row(H_A * LANES),
                   tr(H_A * D_VA), tr(W_B), row(W_B), tr(W_B)),
        out_shape=out_shapes,
        compiler_params=pltpu.CompilerParams(dimension_semantics=("arbitrary", "arbitrary"),
                                             vmem_limit_bytes=VMEM_LIMIT),
        name="pre_proj_prompt",
    )(x2d, tabs, *weights)


def _pre_s_kernel(x_ref, ct_ref, st_ref, w1_ref, wq_ref, gcq_ref, gckv_ref,
                  ckv_ref, krope_ref, ndk_ref, ndv_ref, qa_ref, dq_ref, dk_ref, dv_ref):
    proj = _dot(x_ref[...].astype(BF16), w1_ref[...])
    cqn = _rms(proj[:, 0:D_CQ], gcq_ref[...])
    ckv_ref[...] = _rms(proj[:, D_CQ:D_CQ + D_CKV], gckv_ref[...])
    dq = proj[:, O_PROJ:O_PROJ + W_B]
    dk = proj[:, O_PROJ + W_B:O_PROJ + 2 * W_B]
    dv = proj[:, O_PROJ + 2 * W_B:O_PROJ + 3 * W_B]
    o_kr = O_PROJ + 3 * W_B
    ndk_ref[...] = dk
    ndv_ref[...] = dv
    dq_ref[...] = (dq * SCALE_B).astype(BF16)
    dk_ref[...] = dk.astype(BF16)
    dv_ref[...] = dv.astype(BF16)
    ct = ct_ref[...]
    st = st_ref[...]
    krp = proj[:, o_kr:o_kr + LANES] * ct + proj[:, o_kr + LANES:o_kr + 2 * LANES] * st
    krope_ref[...] = pltpu.roll(krp, LANES - D_NOPE, 1)[:, 0:D_ROPE]
    _rope_q(_dot(cqn.astype(BF16), wq_ref[...]), ct, st, qa_ref)


def _pre_s_call(x2d, ct, st, w1, wq, gcq, gckv, tm):
    t = x2d.shape[0]
    row = lambda c: pl.BlockSpec((tm, c), lambda i: (i, 0))
    tab = pl.BlockSpec((tm, LANES), lambda i: (0, 0))
    out_shapes = (
        jax.ShapeDtypeStruct((t, D_CKV), F32), jax.ShapeDtypeStruct((t, D_ROPE), F32),
        jax.ShapeDtypeStruct((t, W_B), F32), jax.ShapeDtypeStruct((t, W_B), F32),
        jax.ShapeDtypeStruct((t, H_A * LANES), BF16), jax.ShapeDtypeStruct((t, W_B), BF16),
        jax.ShapeDtypeStruct((t, W_B), BF16), jax.ShapeDtypeStruct((t, W_B), BF16))
    return pl.pallas_call(
        _pre_s_kernel,
        grid=(t // tm,),
        in_specs=[row(D_MODEL), tab, tab, _const_spec(w1.shape), _const_spec(wq.shape),
                  _const_spec(gcq.shape), _const_spec(gckv.shape)],
        out_specs=(row(D_CKV), row(D_ROPE), row(W_B), row(W_B), row(H_A * LANES), row(W_B), row(W_B), row(W_B)),
        out_shape=out_shapes,
        compiler_params=pltpu.CompilerParams(dimension_semantics=("arbitrary",),
                                             vmem_limit_bytes=VMEM_LIMIT),
        name="pre_proj_sample",
    )(x2d, ct, st, w1, wq, gcq, gckv)


def _chunk_mask_t():
    r = lax.broadcasted_iota(jnp.int32, (HQ, HQ), 0) // CHUNK
    c = lax.broadcasted_iota(jnp.int32, (HQ, HQ), 1) // CHUNK
    return r <= c


def _rowmax(x):
    return jnp.max(x, -1, keepdims=True)


def _rowsum(x):
    return jnp.sum(x, -1, keepdims=True)


def _colmax(x):
    return jnp.max(x, 0, keepdims=True)


def _colsum(x):
    return jnp.sum(x, 0, keepdims=True)


def _scores_t(k_ref, mi, qt, r0, mask, corr):
    so = _dot(k_ref[mi, 0:r0, :], qt) if r0 > 0 else None
    s0 = _dot(k_ref[mi, r0:r0 + HQ, :], qt)
    s1 = _dot(k_ref[mi, r0 + HQ:r0 + TQ, :], qt[:, HQ:TQ])
    s0l = s0[:, 0:HQ]
    if corr is not None:
        s0l = s0l - corr
        s1 = s1 - corr
    s0 = jnp.concatenate([jnp.where(mask, s0l, -jnp.inf), s0[:, HQ:TQ]], axis=1)
    return so, s0, jnp.where(mask, s1, -jnp.inf)


def _probs_t(sc):
    so, s0, s1 = sc
    m = _colmax(s0)
    if so is not None:
        m = jnp.maximum(m, _colmax(so))
    mr = jnp.maximum(m[:, HQ:TQ], _colmax(s1))
    m = jnp.concatenate([m[:, 0:HQ], mr], axis=1)
    p0 = jnp.exp2(s0 - m)
    p1 = jnp.exp2(s1 - mr)
    l = _colsum(p0) + jnp.concatenate([jnp.zeros((1, HQ), F32), _colsum(p1)], axis=1)
    p_right = jnp.concatenate([p0[:, HQ:TQ], p1], axis=0).astype(BF16)
    p_left = p0[:, 0:HQ].astype(BF16)
    if so is None:
        return None, p_left, p_right, l
    po = jnp.exp2(so - m)
    return po.astype(BF16), p_left, p_right, l + _colsum(po)


def _weighted_values_t(pr, vt_ref, r0):
    po, p_left, p_right, l = pr
    o = jnp.concatenate([_dot(vt_ref[0, :, r0:r0 + HQ], p_left), _dot(vt_ref[0, :, r0:r0 + TQ], p_right)], axis=1)
    if po is not None:
        o = o + _dot(vt_ref[0, :, 0:r0], po)
    return o, l


def _attend_pipelined(n_tiles, k_ref, get_qt, vt_ref, mask, corr, finish, ahead):
    chains = [(qi, mi) for qi in reversed(range(n_tiles)) for mi in range(2)]
    sc, res = {}, {}

    def scores(c):
        qi, mi = chains[c]
        sc[c] = _scores_t(k_ref, mi, get_qt(qi, mi), qi * TQ, mask, corr)

    for c in range(min(ahead, len(chains))):
        scores(c)
    for c, (qi, mi) in enumerate(chains):
        pr = _probs_t(sc.pop(c))
        if c + ahead < len(chains):
            scores(c + ahead)
        res[c] = _weighted_values_t(pr, vt_ref, qi * TQ)
        if mi == 1:
            finish(qi, res.pop(c - 1), res.pop(c))


def _attn_a_kernel(k1_ref, k2_ref, qt_ref, vt_ref, o_ref, k_ref):
    s = k1_ref.shape[1]
    k_ref[0] = k1_ref[0]
    k_ref[1] = k2_ref[0]
    row = lax.broadcasted_iota(jnp.int32, (LANES, TQ), 0)

    def get_qt(qi, mi):
        return qt_ref[0, mi * LANES:(mi + 1) * LANES, qi * TQ:(qi + 1) * TQ]

    def finish(qi, r1, r2):
        ot = jnp.where(row < D_VA, r1[0] / r1[1], r2[0] / r2[1])
        o_ref[0, qi * TQ:(qi + 1) * TQ, :] = ot.T.astype(BF16)

    _attend_pipelined(s // TQ, k_ref, get_qt, vt_ref, _chunk_mask_t(), None, finish, ahead=2)


def _lambda_value(lq1_ref, lk1_ref, lq2_ref, lk2_ref):
    a = jnp.sum(lq1_ref[...] * lk1_ref[...], -1, keepdims=True)
    b = jnp.sum(lq2_ref[...] * lk2_ref[...], -1, keepdims=True)
    return jnp.exp(a) - jnp.exp(b) + LAM_INIT


def _attn_b_kernel(slope_ref, qaug_ref, kaug_ref, lq1_ref, lk1_ref, lq2_ref, lk2_ref, g_ref,
                   dk_ref, dqt_ref, vt_ref, o_ref, k_ref, qt_ref):
    s = dk_ref.shape[1]
    lane = lax.broadcasted_iota(jnp.int32, (s, LANES), 1)
    dk = dk_ref[0]
    k_ref[0] = jnp.where(lane < D_HB, dk, kaug_ref[0])
    k_ref[1] = jnp.where(lane >= D_HB, dk, kaug_ref[1])
    qt_ref[0, 0:D_HB, :] = dqt_ref[0, 0:D_HB, :]
    qt_ref[0, D_HB:LANES, :] = qaug_ref[0]
    qt_ref[1, 0:D_HB, :] = qaug_ref[0]
    qt_ref[1, D_HB:LANES, :] = dqt_ref[0, D_HB:LANES, :]
    slope2 = slope_ref[0, 0:1, 0:1] * (2.0 * LOG2E)
    lam = _lambda_value(lq1_ref, lk1_ref, lq2_ref, lk2_ref)
    ri = lax.broadcasted_iota(jnp.int32, (HQ, HQ), 0)
    ci = lax.broadcasted_iota(jnp.int32, (HQ, HQ), 1)
    corr = slope2 * jnp.maximum(ri - ci, 0).astype(F32)

    def get_qt(qi, mi):
        return qt_ref[mi, :, qi * TQ:(qi + 1) * TQ]

    def finish(qi, r1, r2):
        o = (r1[0] / r1[1] - lam * (r2[0] / r2[1])).T
        o_ref[0, qi * TQ:(qi + 1) * TQ, :] = (_rms(o, g_ref[...]) * (1.0 - LAM_INIT)).astype(BF16)

    _attend_pipelined(s // TQ, k_ref, get_qt, vt_ref, _chunk_mask_t(), corr, finish, ahead=2)


def _attn_a_call(ka, qat, vat):
    b, s, _ = ka.shape
    blk = lambda f: pl.BlockSpec((1, s, LANES), f)
    return pl.pallas_call(
        _attn_a_kernel,
        grid=(b, H_A // 2),
        in_specs=[blk(lambda i, j: (i, 0, 2 * j)), blk(lambda i, j: (i, 0, 2 * j + 1)),
                  pl.BlockSpec((1, 2 * LANES, s), lambda i, j: (i, j, 0)),
                  pl.BlockSpec((1, LANES, s), lambda i, j: (i, j, 0))],
        out_specs=blk(lambda i, j: (i, 0, j)),
        out_shape=jax.ShapeDtypeStruct((b, s, H_A * D_VA), BF16),
        scratch_shapes=[pltpu.VMEM((2, s, LANES), BF16)],
        compiler_params=pltpu.CompilerParams(dimension_semantics=("arbitrary", "arbitrary"),
                                             vmem_limit_bytes=VMEM_LIMIT),
        name="attn_mla",
    )(ka, ka, qat, vat)


def _attn_b_call(slopes, qaug, kaug, lq1, lk1, lq2, lk2, g, dk, dqt, dvt):
    b, s, _ = dk.shape
    blk = pl.BlockSpec((1, s, LANES), lambda i, j: (i, 0, j))
    blkt = pl.BlockSpec((1, LANES, s), lambda i, j: (i, j, 0))
    vec = lambda a: pl.BlockSpec(a.shape, lambda i, j: (0,) * a.ndim)
    return pl.pallas_call(
        _attn_b_kernel,
        grid=(b, H_B),
        in_specs=[pl.BlockSpec((1, 8, LANES), lambda i, j: (j, 0, 0)),
                  pl.BlockSpec((1, D_HB, s), lambda i, j: (j, 0, 0)),
                  vec(kaug), vec(lq1), vec(lk1), vec(lq2), vec(lk2), vec(g),
                  blk, blkt, blkt],
        out_specs=blk,
        out_shape=jax.ShapeDtypeStruct((b, s, W_B), BF16),
        scratch_shapes=[pltpu.VMEM((2, s, LANES), BF16), pltpu.VMEM((2, LANES, s), BF16)],
        compiler_params=pltpu.CompilerParams(dimension_semantics=("arbitrary", "arbitrary"),
                                             vmem_limit_bytes=VMEM_LIMIT),
        name="attn_diff",
    )(slopes, qaug, kaug, lq1, lk1, lq2, lk2, g, dk, dqt, dvt)


def _alibi_slope(h):
    return 2.0 ** (-8.0 * (h + 1) / H_B)


def _stack_rows(x, reps, width):
    n, c = x.shape
    t = jnp.concatenate([x] * reps, axis=0)
    rb = lax.broadcasted_iota(jnp.int32, (reps * n, c), 0) // n
    cb = lax.broadcasted_iota(jnp.int32, (reps * n, c), 1) // width
    return jnp.where(rb == cb, t, jnp.zeros_like(t))


def _softmax2(sc, sn):
    m = jnp.maximum(_rowmax(sc), _rowmax(sn))
    pc = jnp.exp2(sc - m)
    pn = jnp.exp2(sn - m)
    l = _rowsum(pc) + _rowsum(pn)
    return pc.astype(BF16), pn.astype(BF16), l


def _attn_s_kernel(past_len, qa_ref, dq_ref, cckv_ref, ckrt_ref, cdkt_ref, cdv4_ref,
                   nckv_ref, nkr_ref, ndk_ref, ndv_ref, wabs_ref, wuvb_ref,
                   lq1_ref, lk1_ref, lq2_ref, lk2_ref, g_ref, oa_ref, ob_ref):
    n = qa_ref.shape[1]
    rows = 2 * H_B * n
    qblk = _stack_rows(qa_ref[0], H_A, LANES)
    g = _dot(qblk, wabs_ref[...])
    qabs = g[:, 0:D_CKV].astype(BF16)
    qrp = g[:, D_CKV:D_CKV + D_ROPE].astype(BF16)
    cc = cckv_ref[0].astype(BF16)
    cn = nckv_ref[0].astype(BF16)
    sca = _dot_nt(qabs, cc) + _dot(qrp, ckrt_ref[0].astype(BF16))
    sna = _dot_nt(qabs, cn) + _dot_nt(qrp, nkr_ref[0].astype(BF16))
    qb = _stack_rows(dq_ref[0], 2 * H_B, D_HB)
    scb = _dot(qb, cdkt_ref[0].astype(BF16))
    snb = _dot_nt(qb, ndk_ref[0])
    pca, pna, la = _softmax2(sca, sna)
    r = lax.broadcasted_iota(jnp.int32, (rows, 1), 0)
    hh = r // (2 * n)
    slope = jnp.zeros((rows, 1), F32)
    for h in range(H_B):
        slope = jnp.where(hh == h, _alibi_slope(h) * LOG2E, slope)
    qpos = (r % n).astype(F32)
    kc = lax.broadcasted_iota(jnp.int32, (1, past_len), 1).astype(F32)
    kn = lax.broadcasted_iota(jnp.int32, (1, n), 1).astype(F32)
    scb = scb - slope * jnp.abs((qpos + float(past_len)) - kc)
    snb = snb - slope * jnp.abs(qpos - kn)
    ctx = ((_dot(pca, cc) + _dot(pna, cn)) / la).astype(BF16)
    pcb, pnb, lb = _softmax2(scb, snb)
    oa = _dot(ctx[0:n], wuvb_ref[0:D_CKV, :])
    for h in range(1, H_A):
        oa = oa + _dot(ctx[h * n:(h + 1) * n], wuvb_ref[h * D_CKV:(h + 1) * D_CKV, :])
    oa_ref[0] = oa.astype(BF16)
    lam = _lambda_value(lq1_ref, lk1_ref, lq2_ref, lk2_ref)
    inv_l = 1.0 / lb
    for h in range(H_B):
        sl = slice(h * LANES, (h + 1) * LANES)
        rs = slice(2 * h * n, (2 * h + 2) * n)
        vh = cdv4_ref[0, pl.ds(h, past_len, stride=H_B), :].astype(BF16)
        of = (_dot(pcb[rs], vh) + _dot(pnb[rs], ndv_ref[0, :, sl])) * inv_l[rs]
        o = of[0:n] - lam * of[n:2 * n]
        ob_ref[0, :, sl] = (_rms(o, g_ref[...]) * (1.0 - LAM_INIT)).astype(BF16)


def _attn_s_call(qa, dq, cckv, ckrt, cdkt, cdv4, nckv, nkr, ndk, ndv, wabs, wuvb, lq1, lk1, lq2, lk2, g):
    b, n, _ = qa.shape
    past_len = cckv.shape[1]
    per_b = lambda a: pl.BlockSpec((1,) + a.shape[1:], lambda i: (i, 0, 0))
    vec = lambda a: pl.BlockSpec(a.shape, lambda i: (0,) * a.ndim)
    args = (qa, dq, cckv, ckrt, cdkt, cdv4, nckv, nkr, ndk, ndv)
    consts = (wabs, wuvb)
    vecs = (lq1, lk1, lq2, lk2, g)
    return pl.pallas_call(
        functools.partial(_attn_s_kernel, past_len),
        grid=(b,),
        in_specs=[per_b(a) for a in args] + [_const_spec(a.shape) for a in consts] + [vec(a) for a in vecs],
        out_specs=(pl.BlockSpec((1, n, H_A * D_VA), lambda i: (i, 0, 0)),
                   pl.BlockSpec((1, n, W_B), lambda i: (i, 0, 0))),
        out_shape=(jax.ShapeDtypeStruct((b, n, H_A * D_VA), BF16), jax.ShapeDtypeStruct((b, n, W_B), BF16)),
        compiler_params=pltpu.CompilerParams(dimension_semantics=("arbitrary",),
                                             vmem_limit_bytes=VMEM_LIMIT),
        name="attn_sample",
    )(*args, *consts, *vecs)


def _post_kernel(nb, tiles_per_seq, x_ref, oa_ref, ob_ref, p_ref, state_ref,
                 woa_ref, wob_ref, ln1g_ref, ln1b_ref, wup_ref, cw_ref, cb_ref, wdn_ref,
                 ln2g_ref, ln2b_ref, wg_ref, bg_ref, wp_ref,
                 y_ref, nconv_ref, carry_ref, ush_ref, act_ref):
    tm = x_ref.shape[0]
    s = tm // nb
    i = pl.program_id(0)

    @pl.when(i % tiles_per_seq == 0)
    def _():
        carry_ref[...] = state_ref[...]

    a = _dot(oa_ref[...], woa_ref[...]) + _dot(ob_ref[...], wob_ref[...])
    h = _ln(ALPHA * x_ref[...] + a, ln1g_ref[...], ln1b_ref[...])
    hb = h.astype(BF16)

    def conv(c0):
        u = _dot(hb, wup_ref[:, c0:c0 + FC])
        u3 = u.reshape(nb, s, FC)
        tail = u3[:, s - 2:s, :]
        z = cb_ref[:, c0:c0 + FC] + u3 * cw_ref[2:3, c0:c0 + FC]
        zs = []
        for k in range(FC // LANES):
            cl = slice(c0 + k * LANES, c0 + (k + 1) * LANES)
            sl = slice(k * LANES, (k + 1) * LANES)
            for b in range(nb):
                ush_ref[k, b, pl.ds(PITCH * (HALO - 2), 2, stride=PITCH), :] = carry_ref[b, :, cl]
                ush_ref[k, b, pl.ds(PITCH * HALO, s, stride=PITCH), :] = u3[b, :, sl]
            um2 = jnp.stack([ush_ref[k, b, pl.ds(PITCH * (HALO - 2), s, stride=PITCH), :] for b in range(nb)])
            um1 = jnp.stack([ush_ref[k, b, pl.ds(PITCH * (HALO - 1), s, stride=PITCH), :] for b in range(nb)])
            zs.append(z[:, :, sl] + um2 * cw_ref[0:1, cl] + um1 * cw_ref[1:2, cl])
        carry_ref[:, :, c0:c0 + FC] = tail
        nconv_ref[:, :, c0:c0 + FC] = tail
        return jnp.concatenate(zs, axis=-1).reshape(tm, FC)

    for c in range(N_FC):
        zg = conv(c * FC)
        zv = conv(D_FF + c * FC)
        act_ref[:, c * FC:(c + 1) * FC] = (_gelu(zg) * zv).astype(BF16)

    f = _dot(act_ref[...], wdn_ref[...])
    h2 = _ln(ALPHA * h + f, ln2g_ref[...], ln2b_ref[...])
    gate = jax.nn.sigmoid(_dot(h2.astype(BF16), wg_ref[...]) + bg_ref[...])
    y_ref[...] = h2 + gate * _dot(p_ref[...].astype(BF16), wp_ref[...])


def _post_call(x2d, oa, ob, p2d, state, weights, tm, nb, tiles_per_seq):
    t = x2d.shape[0]
    n_seq = state.shape[0]
    s = tm // nb
    row = lambda c: pl.BlockSpec((tm, c), lambda i: (i, 0))
    st_spec = pl.BlockSpec((nb, CONV_W - 1, 2 * D_FF), lambda i: (i // tiles_per_seq, 0, 0))
    return pl.pallas_call(
        functools.partial(_post_kernel, nb, tiles_per_seq),
        grid=(t // tm,),
        in_specs=[row(D_MODEL), row(H_A * D_VA), row(W_B), row(D_PLE), st_spec]
                 + [_const_spec(w.shape) for w in weights],
        out_specs=(row(D_MODEL), st_spec),
        out_shape=(jax.ShapeDtypeStruct((t, D_MODEL), F32),
                   jax.ShapeDtypeStruct((n_seq, CONV_W - 1, 2 * D_FF), F32)),
        scratch_shapes=[pltpu.VMEM((nb, CONV_W - 1, 2 * D_FF), F32),
                        pltpu.VMEM((FC // LANES, nb, PITCH * (s + HALO), LANES), F32),
                        pltpu.VMEM((tm, D_FF), BF16)],
        compiler_params=pltpu.CompilerParams(dimension_semantics=("arbitrary",),
                                             vmem_limit_bytes=VMEM_LIMIT),
        name="post_ffn",
    )(x2d, oa, ob, p2d, state, *weights)


def _rope_angles(pos):
    half = D_ROPE // 2
    inv = 1.0 / (ROPE_BASE ** (jnp.arange(half, dtype=F32) / half))
    ang = pos.astype(F32)[:, None] * inv[None, :]
    return jnp.cos(ang), jnp.sin(ang)


def _rope_tables(pos):
    cos, sin = _rope_angles(pos)
    n = pos.shape[0]
    ct = jnp.concatenate([jnp.ones((n, D_NOPE), F32), cos, cos, jnp.zeros((n, LANES - D_NOPE - D_ROPE), F32)], 1)
    st = jnp.concatenate([jnp.zeros((n, D_NOPE), F32), sin, sin, jnp.zeros((n, LANES - D_NOPE - D_ROPE), F32)], 1)
    return ct, st


def _rope_table_t(pos):
    cos, sin = _rope_angles(pos)
    return jnp.concatenate([cos.T, sin.T], 0)


def _rot_half_cols(w):
    half = D_ROPE // 2
    return jnp.concatenate([-w[..., half:], w[..., :half]], -1)


def _prep_weights(w_in, w_uq, w_uk, w_uv):
    o1, o2, o3 = D_CQ, D_CQ + D_CKV, D_CQ + D_CKV + D_ROPE
    o4, o5 = o3 + W_B, o3 + 2 * W_B
    w_kr = w_in[:, o2:o3]
    w_dq, w_dk, w_dv = w_in[:, o3:o4], w_in[:, o4:o5], w_in[:, o5:]
    z = lambda c: jnp.zeros((D_MODEL, c), F32)
    w1p = jnp.concatenate([w_in[:, :o2], w_dv], 1).astype(BF16)
    wtp = jnp.concatenate([w_dq, w_dk, w_kr], 1).T.astype(BF16)
    w1s = jnp.concatenate([w_in[:, :o2], w_dq, w_dk, w_dv,
                           z(D_NOPE), w_kr, z(LANES - D_NOPE - D_ROPE),
                           z(D_NOPE), _rot_half_cols(w_kr), z(LANES - D_NOPE - D_ROPE)], 1).astype(BF16)
    wq3 = w_uq.reshape(D_CQ, H_A, D_NOPE + D_ROPE)
    zq = lambda c: jnp.zeros((D_CQ, H_A, c), F32)
    wq_a = jnp.concatenate([wq3, zq(LANES - D_NOPE - D_ROPE)], -1).reshape(D_CQ, H_A * LANES)
    wq_b = jnp.concatenate([zq(D_NOPE), _rot_half_cols(wq3[..., D_NOPE:]), zq(LANES - D_NOPE - D_ROPE)],
                           -1).reshape(D_CQ, H_A * LANES)
    wq = jnp.concatenate([wq_a, wq_b], 1).astype(BF16)
    wk3 = w_uk.reshape(D_CKV, H_A, D_NOPE)
    wabs_lat = jnp.concatenate([jnp.transpose(wk3, (1, 2, 0)),
                                jnp.zeros((H_A, LANES - D_NOPE, D_CKV), F32)], 1)
    pick = jnp.zeros((LANES, LANES), F32).at[D_NOPE + jnp.arange(D_ROPE), jnp.arange(D_ROPE)].set(1.0)
    wabs = jnp.concatenate([wabs_lat, jnp.broadcast_to(pick, (H_A, LANES, LANES))], -1)
    wabs = wabs.reshape(H_A * LANES, D_CKV + LANES).astype(BF16)
    wv3 = w_uv.reshape(D_CKV, H_A, D_VA)
    eye = jnp.eye(H_A, dtype=F32)
    wuvb = jnp.einsum('lhd,hg->hlgd', wv3, eye).reshape(H_A * D_CKV, H_A * D_VA).astype(BF16)
    return w1p, wtp, w1s, wq, wq_a.T.astype(BF16), w_uk.T.astype(BF16), w_uv.T.astype(BF16), wabs, wuvb


def _alibi_tables(s):
    j = jnp.arange(s, dtype=jnp.int32)
    j_hi = ((j // POS_SPLIT) * POS_SPLIT).astype(F32)
    j_lo = (j % POS_SPLIT).astype(F32)
    cols = jnp.stack([j_hi, j_lo] * 3, axis=1)
    kaug = jnp.zeros((2, s, LANES), F32)
    kaug = kaug.at[0, :, D_HB:D_HB + N_AUG].set(cols).at[1, :, 0:N_AUG].set(cols).astype(BF16)
    c_hi = jnp.asarray(LOG2E, F32).astype(BF16).astype(F32)
    c_mid = (jnp.asarray(LOG2E, F32) - c_hi).astype(BF16).astype(F32)
    c_lo = (jnp.asarray(LOG2E, F32) - c_hi - c_mid).astype(BF16).astype(F32)
    consts = jnp.stack([c_hi, c_hi, c_mid, c_mid, c_lo, c_lo])
    slopes = jnp.asarray([_alibi_slope(h) for h in range(H_B)], F32)
    vals = slopes[:, None] * consts[None, :]
    qaug = jnp.zeros((H_B, D_HB, s), F32)
    qaug = qaug.at[:, 0:N_AUG, :].set(jnp.broadcast_to(vals[:, :, None], (H_B, N_AUG, s))).astype(BF16)
    return kaug, qaug, jnp.broadcast_to(slopes[:, None, None], (H_B, 8, LANES))


def kernel(x_prompt, x_sample, cache_ckv, cache_krope, cache_diff_k, cache_diff_v, state_ffn_conv, p_prompt, p_sample, w_in, g_cq, w_uq, g_ckv, w_uk, w_uv, lambda_q1, lambda_k1, lambda_q2, lambda_k2, g_subln, w_o, ln1_g, ln1_b, w_up, conv_w, conv_b, w_down, ln2_g, ln2_b, w_ple_gate, b_ple_gate, w_ple_proj):
    b, s, _ = x_prompt.shape
    bs, n, _ = x_sample.shape
    past_len = cache_ckv.shape[2]
    assert w_in.shape[0] == DEPTH == 1
    assert s % TQ == 0 and s % TM == 0 and HQ % CHUNK == 0
    assert past_len % CHUNK == 0 and CONV_W - 1 <= n <= CHUNK and bs % SAMPLE_SEQS == 0
    w1p, wtp, w1s, wq, wqt, wk, wuvt, wabs, wuvb = _prep_weights(w_in[0], w_uq[0], w_uk[0], w_uv[0])
    post_w = (w_o[0][:H_A * D_VA].astype(BF16), w_o[0][H_A * D_VA:].astype(BF16), ln1_g, ln1_b,
              w_up[0].astype(BF16), conv_w[0], conv_b, w_down[0].astype(BF16), ln2_g, ln2_b,
              w_ple_gate[0].astype(BF16), b_ple_gate, w_ple_proj[0].astype(BF16))
    kaug, qaug, slopes = _alibi_tables(s)
    lams = (lambda_q1, lambda_k1, lambda_q2, lambda_k2)

    x2p = x_prompt.reshape(b * s, D_MODEL)
    (ckv_p, krt_p, dkt_p, dv4_p, qat, ka, vat, dqt, dk, dvt) = _pre_t_call(
        x2p, b, s, _rope_table_t(jnp.arange(s)), (w1p, wtp, wqt, wk, wuvt, g_cq, g_ckv), TM)
    r3 = lambda a: a.reshape(b, s, a.shape[-1])
    oa = _attn_a_call(r3(ka), qat, vat)
    ob = _attn_b_call(slopes, qaug, kaug, *lams, g_subln, r3(dk), dqt, dvt)
    zero_state = jnp.zeros((b, CONV_W - 1, 2 * D_FF), F32)
    y_p, conv_p = _post_call(x2p, oa.reshape(b * s, -1), ob.reshape(b * s, -1),
                             p_prompt[0].reshape(b * s, D_PLE), zero_state, post_w, TM, 1, s // TM)

    nbs = SAMPLE_SEQS
    tms = nbs * n
    cts, sts = _rope_tables(past_len + jnp.arange(n))
    cts, sts = jnp.tile(cts, (nbs, 1)), jnp.tile(sts, (nbs, 1))
    x2s = x_sample.reshape(bs * n, D_MODEL)
    (ckv_s, kr_s, ndk_s, ndv_s, qa_s, dq_s, dk_s, dv_s) = _pre_s_call(x2s, cts, sts, w1s, wq, g_cq, g_ckv, tms)
    q3 = lambda a: a.reshape(bs, n, a.shape[-1])
    ckrt = jnp.transpose(cache_krope[0], (0, 2, 1))
    cdkt = jnp.transpose(cache_diff_k[0], (0, 2, 3, 4, 1)).reshape(bs, W_B, past_len)
    cdv4 = cache_diff_v[0].reshape(bs, past_len * H_B, 2 * D_HB)
    oa_s, ob_s = _attn_s_call(
        q3(qa_s), q3(dq_s), cache_ckv[0], ckrt, cdkt, cdv4,
        q3(ckv_s), q3(kr_s), q3(dk_s), q3(dv_s), wabs, wuvb, *lams, g_subln)
    y_s, conv_s = _post_call(x2s, oa_s.reshape(bs * n, -1), ob_s.reshape(bs * n, -1),
                             p_sample[0].reshape(bs * n, D_PLE), state_ffn_conv[0], post_w, tms, nbs, 1)

    new_kr_p = jnp.transpose(krt_p, (0, 2, 1))[None]
    new_dk_p = jnp.transpose(dkt_p.reshape(b, H_B, 2, D_HB, s), (0, 4, 1, 2, 3))[None]
    return (y_p.reshape(b, s, D_MODEL), y_s.reshape(bs, n, D_MODEL),
            ckv_p.reshape(1, b, s, D_CKV), new_kr_p, new_dk_p,
            dv4_p.reshape(1, b, s, H_B, 2 * D_HB),
            conv_p[None],
            ckv_s.reshape(1, bs, n, D_CKV), kr_s.reshape(1, bs, n, D_ROPE),
            ndk_s.reshape(1, bs, n, H_B, 2, D_HB), ndv_s.reshape(1, bs, n, H_B, 2 * D_HB),
            conv_s[None])
```
